```python
import math
import jax, jax.numpy as jnp
from jax import lax
import numpy as np

D_MODEL = 1024
BATCH = 4
SEQ = 4096
DEPTH = 1

HEAD_DIM = 64
SC_WIDTH = D_MODEL
SC_HEADS = SC_WIDTH // HEAD_DIM
SC_WIN = 3
SSM_WIDTH = D_MODEL
SSM_HEADS = SSM_WIDTH // HEAD_DIM
SSM_GROUPS = 2
SSM_STATE = 128
SSM_CONV = 4
SSM_CHUNK = 128
SSM_GN = SSM_GROUPS * SSM_STATE
SSM_CONV_DIM = SSM_WIDTH + 2 * SSM_GN
MIX_WIDTH = SC_WIDTH + SSM_WIDTH
IN_WIDTH = 3 * SC_WIDTH + SSM_WIDTH + SSM_CONV_DIM + SSM_HEADS
IN_SPLITS = [SC_WIDTH, 2 * SC_WIDTH, 3 * SC_WIDTH, 3 * SC_WIDTH + SSM_WIDTH,
             3 * SC_WIDTH + SSM_WIDTH + SSM_CONV_DIM]
PEER_HEADS = 8
PEER_NKEYS = 128
PEER_EXPERTS = PEER_NKEYS * PEER_NKEYS
PEER_TOPK = 16
PEER_DKEY = 256
PEER_HALF = PEER_DKEY // 2
PEER_BLOCK = 128
DEEPNORM_ALPHA = (2.0 * DEPTH) ** 0.25
DEEPNORM_BETA = (8.0 * DEPTH) ** -0.25
LN_EPS = 1e-5
RMS_EPS = 1e-6

kernel_name = "hymba_conv_ssd_peer_deepnorm_adaln"


def layer_norm(x, g, b):
    xf = x.astype(jnp.float32)
    mu = jnp.mean(xf, -1, keepdims=True)
    var = jnp.mean(jnp.square(xf - mu), -1, keepdims=True)
    return ((xf - mu) * lax.rsqrt(var + LN_EPS) * g + b).astype(x.dtype)


def group_rms_norm(x, w, n_groups):
    shp = x.shape
    xf = x.astype(jnp.float32).reshape(shp[:-1] + (n_groups, shp[-1] // n_groups))
    xf = xf * lax.rsqrt(jnp.mean(jnp.square(xf), -1, keepdims=True) + RMS_EPS)
    return (xf.reshape(shp) * w).astype(x.dtype)


def causal_dwconv(x, w):
    k, ch = w.shape
    return lax.conv_general_dilated(x, w[:, None, :], window_strides=(1,), padding=[(k - 1, 0)],
                                    dimension_numbers=('NWC', 'WIO', 'NWC'), feature_group_count=ch)


def ssd_chunked(x, dt, a, b_in, c_in):
    bsz, seqlen, n_heads, hd = x.shape
    g = b_in.shape[2]
    r = n_heads // g
    L = SSM_CHUNK
    nc = seqlen // L
    xdt = (x * dt[..., None].astype(x.dtype)).reshape(bsz, nc, L, g, r, hd)
    a_dt = (dt * a).reshape(bsz, nc, L, g, r)
    a_dt = jnp.transpose(a_dt, (0, 3, 4, 1, 2))
    bc = b_in.reshape(bsz, nc, L, g, -1)
    cc = c_in.reshape(bsz, nc, L, g, -1)
    a_cum = jnp.cumsum(a_dt, -1)
    seg = a_cum[..., :, None] - a_cum[..., None, :]
    causal = jnp.tril(jnp.ones((L, L), dtype=bool))
    decay_in = jnp.where(causal, jnp.exp(jnp.where(causal, seg, 0.0)), 0.0).astype(x.dtype)
    cb = jnp.einsum('bclgn,bcsgn->bgcls', cc, bc)
    y_diag = jnp.einsum('bgcls,bgrcls,bcsgrp->bclgrp', cb, decay_in, xdt)
    decay_to_end = jnp.exp(a_cum[..., -1:] - a_cum).astype(x.dtype)
    chunk_states = jnp.einsum('bclgn,bgrcl,bclgrp->cbgrpn', bc, decay_to_end, xdt)
    chunk_decay = jnp.moveaxis(jnp.exp(a_cum[..., -1]), -1, 0).astype(x.dtype)

    def step(h, inp):
        s_c, d_c = inp
        return h * d_c[..., None, None] + s_c, h

    h0 = jnp.zeros(chunk_states.shape[1:], x.dtype)
    _, prev_states = lax.scan(step, h0, (chunk_states, chunk_decay))
    decay_from_start = jnp.exp(a_cum).astype(x.dtype)
    y_off = jnp.einsum('bclgn,cbgrpn,bgrcl->bclgrp', cc, prev_states, decay_from_start)
    return (y_diag + y_off).reshape(bsz, seqlen, n_heads, hd)


def hybrid_mixer(u, w_in, sc_conv_w, ssm_conv_w, ssm_conv_b, dt_bias, a_log, d_skip,
                 sc_norm_w, ssm_norm_w, w_out):
    bsz, seqlen, _ = u.shape
    proj = u @ w_in
    sc_b, sc_c, sc_h, z, xbc, dt = jnp.split(proj, IN_SPLITS, axis=-1)
    y_sc = sc_b * causal_dwconv(sc_c * sc_h, sc_conv_w)
    y_sc = group_rms_norm(y_sc, sc_norm_w, SC_HEADS)
    xbc = jax.nn.silu(causal_dwconv(xbc, ssm_conv_w) + ssm_conv_b)
    xs, bs, cs = jnp.split(xbc, [SSM_WIDTH, SSM_WIDTH + SSM_GN], axis=-1)
    dt = jax.nn.softplus(dt.astype(jnp.float32) + dt_bias.astype(jnp.float32))
    a = -jnp.exp(a_log.astype(jnp.float32))
    xh = xs.reshape(bsz, seqlen, SSM_HEADS, HEAD_DIM)
    y = ssd_chunked(xh, dt, a,
                    bs.reshape(bsz, seqlen, SSM_GROUPS, SSM_STATE),
                    cs.reshape(bsz, seqlen, SSM_GROUPS, SSM_STATE))
    y = (y + xh * d_skip[:, None]).reshape(bsz, seqlen, SSM_WIDTH)
    y_ssm = group_rms_norm(y * jax.nn.silu(z), ssm_norm_w, SSM_GROUPS)
    return jnp.concatenate([y_sc, y_ssm], axis=-1) @ w_out


def peer_ffn(u, w_query, sub_keys1, sub_keys2, expert_u, expert_v):
    bsz, seqlen, d = u.shape
    q = (u @ w_query).reshape(bsz, seqlen, PEER_HEADS, 2, PEER_HALF)
    s1 = jnp.einsum('bshk,hnk->bshn', q[..., 0, :], sub_keys1).astype(jnp.float32)
    s2 = jnp.einsum('bshk,hnk->bshn', q[..., 1, :], sub_keys2).astype(jnp.float32)
    v1, i1 = lax.top_k(s1, PEER_TOPK)
    v2, i2 = lax.top_k(s2, PEER_TOPK)
    cand = (v1[..., :, None] + v2[..., None, :]).reshape(bsz, seqlen, PEER_HEADS, PEER_TOPK * PEER_TOPK)
    top_s, top_ij = lax.top_k(cand, PEER_TOPK)
    e_idx = (jnp.take_along_axis(i1, top_ij // PEER_TOPK, axis=-1) * PEER_NKEYS
             + jnp.take_along_axis(i2, top_ij % PEER_TOPK, axis=-1))
    gate = jax.nn.softmax(top_s, axis=-1).astype(u.dtype)
    nb = (bsz * seqlen) // PEER_BLOCK
    ub = u.reshape(nb, PEER_BLOCK, d)
    eb = e_idx.reshape(nb, PEER_BLOCK, PEER_HEADS, PEER_TOPK)
    gb = gate.reshape(nb, PEER_BLOCK, PEER_HEADS, PEER_TOPK)

    def block(args):
        xt, et, gt = args
        h = jnp.einsum('td,thkd->thk', xt, expert_u[et])
        act = jax.nn.gelu(h, approximate=False) * gt
        return jnp.einsum('thk,thkd->td', act, expert_v[et])

    return lax.map(block, (ub, eb, gb)).reshape(bsz, seqlen, d)


def setup_inputs(seed: int = 0) -> dict:
    key = jax.random.key(seed)
    ks = jax.random.split(key, 24)
    f32 = jnp.float32
    nrm = lambda k, shp, s: jax.random.normal(k, shp, f32) * s
    dt0 = jnp.exp(jax.random.uniform(ks[9], (DEPTH, SSM_HEADS), f32, math.log(1e-3), math.log(1e-1)))
    return {
        "x": nrm(ks[0], (BATCH, SEQ, D_MODEL), 1.0),
        "c": nrm(ks[1], (BATCH, D_MODEL), 1.0),
        "w_cond": nrm(ks[2], (DEPTH, D_MODEL, 6 * D_MODEL), 0.2 * D_MODEL ** -0.5),
        "b_cond": nrm(ks[3], (DEPTH, 6 * D_MODEL), 0.02),
        "w_in": nrm(ks[4], (DEPTH, D_MODEL, IN_WIDTH), D_MODEL ** -0.5),
        "sc_conv_w": nrm(ks[5], (DEPTH, SC_WIN, SC_WIDTH), SC_WIN ** -0.5),
        "ssm_conv_w": nrm(ks[6], (DEPTH, SSM_CONV, SSM_CONV_DIM), SSM_CONV ** -0.5),
        "ssm_conv_b": nrm(ks[7], (DEPTH, SSM_CONV_DIM), 0.02),
        "dt_bias": dt0 + jnp.log(-jnp.expm1(-dt0)),
        "a_log": jnp.log(jax.random.uniform(ks[10], (DEPTH, SSM_HEADS), f32, 1.0, 16.0)),
        "d_skip": 1.0 + nrm(ks[11], (DEPTH, SSM_HEADS), 0.02),
        "sc_norm_w": 1.0 + nrm(ks[12], (DEPTH, SC_WIDTH), 0.02),
        "ssm_norm_w": 1.0 + nrm(ks[13], (DEPTH, SSM_WIDTH), 0.02),
        "w_out": nrm(ks[14], (DEPTH, MIX_WIDTH, D_MODEL), DEEPNORM_BETA * MIX_WIDTH ** -0.5),
        "ln1_g": 1.0 + nrm(ks[15], (DEPTH, D_MODEL), 0.02),
        "ln1_b": nrm(ks[16], (DEPTH, D_MODEL), 0.02),
        "w_query": nrm(ks[17], (DEPTH, D_MODEL, PEER_HEADS * PEER_DKEY), D_MODEL ** -0.5),
        "sub_keys1": nrm(ks[18], (DEPTH, PEER_HEADS, PEER_NKEYS, PEER_HALF), PEER_HALF ** -0.5),
        "sub_keys2": nrm(ks[19], (DEPTH, PEER_HEADS, PEER_NKEYS, PEER_HALF), PEER_HALF ** -0.5),
        "expert_u": nrm(ks[20], (DEPTH, PEER_EXPERTS, D_MODEL), D_MODEL ** -0.5),
        "expert_v": nrm(ks[21], (DEPTH, PEER_EXPERTS, D_MODEL), DEEPNORM_BETA),
        "ln2_g": 1.0 + nrm(ks[22], (DEPTH, D_MODEL), 0.02),
        "ln2_b": nrm(ks[23], (DEPTH, D_MODEL), 0.02),
    }


def reference(x, c, w_cond, b_cond, w_in, sc_conv_w, ssm_conv_w, ssm_conv_b, dt_bias, a_log,
              d_skip, sc_norm_w, ssm_norm_w, w_out, ln1_g, ln1_b, w_query, sub_keys1, sub_keys2,
              expert_u, expert_v, ln2_g, ln2_b):
    c_act = jax.nn.silu(c)
    for l in range(DEPTH):
        cond = (c_act @ w_cond[l] + b_cond[l])[:, None, :]
        shift1, scale1, gate1, shift2, scale2, gate2 = jnp.split(cond, 6, axis=-1)
        u1 = x * (1.0 + scale1) + shift1
        mix = hybrid_mixer(u1, w_in[l], sc_conv_w[l], ssm_conv_w[l], ssm_conv_b[l], dt_bias[l],
                           a_log[l], d_skip[l], sc_norm_w[l], ssm_norm_w[l], w_out[l])
        x = layer_norm(DEEPNORM_ALPHA * x + (1.0 + gate1) * mix, ln1_g[l], ln1_b[l])
        u2 = x * (1.0 + scale2) + shift2
        ffn = peer_ffn(u2, w_query[l], sub_keys1[l], sub_keys2[l], expert_u[l], expert_v[l])
        x = layer_norm(DEEPNORM_ALPHA * x + (1.0 + gate2) * ffn, ln2_g[l], ln2_b[l])
    return x
```

```python
import functools

import jax
import jax.numpy as jnp
from jax import lax
from jax.experimental import pallas as pl
from jax.experimental.pallas import tpu as pltpu

F32 = jnp.float32
BF16 = jnp.bfloat16

D_MODEL = 1024
HEAD_DIM = 64
SC_HEADS = 16
SC_WIN = 3
SSM_HEADS = 16
SSM_GROUPS = 2
SSM_STATE = 128
SSM_CONV = 4
SSM_CHUNK = 128
SSM_GN = SSM_GROUPS * SSM_STATE
SSM_CONV_DIM = D_MODEL + 2 * SSM_GN
GROUP_W = D_MODEL // SSM_GROUPS
PEER_HEADS = 8
PEER_NKEYS = 128
PEER_TOPK = 16
PEER_HALF = 128
PEER_DKEY = 256
LN_EPS = 1e-5
RMS_EPS = 1e-6
LANES = 128
CARRY = 8
NEG_INF = float("-inf")

MIX_TS = 256
PREP_TM = 256
PEER_TM = 512
PEER_TE = 1024
VMEM_LIMIT = 56 * 1024 * 1024


def _dot(a, b):
    return jnp.dot(a, b, preferred_element_type=F32)


def _dot_nt(a, b):
    return lax.dot_general(a, b, (((1,), (1,)), ((), ())), preferred_element_type=F32)


def _dot_tn(a, b):
    return lax.dot_general(a, b, (((0,), (0,)), ((), ())), preferred_element_type=F32)


def _split_dot_lhs(a, m, terms):
    out = None
    r = a
    for _ in range(terms):
        hi = r.astype(BF16)
        part = _dot(hi, m)
        out = part if out is None else out + part
        r = r - hi.astype(F32)
    return out


def _split_dot_rhs(m, a, terms):
    out = None
    r = a
    for _ in range(terms):
        hi = r.astype(BF16)
        part = _dot(m, hi)
        out = part if out is None else out + part
        r = r - hi.astype(F32)
    return out


def _sigmoid(x):
    return 1.0 / (1.0 + jnp.exp(-x))


def _silu(x):
    return x * _sigmoid(x)


def _softplus(x):
    return jnp.maximum(x, 0.0) + jnp.log(1.0 + jnp.exp(-jnp.abs(x)))


def _layer_norm(v, g, b):
    mu = jnp.mean(v, axis=-1, keepdims=True)
    d = v - mu
    var = jnp.mean(d * d, axis=-1, keepdims=True)
    return d * lax.rsqrt(var + LN_EPS) * g + b


def _cond_kernel(c_ref, w_ref, b_ref, o_ref):
    c = c_ref[...]
    o_ref[...] = jnp.dot(_silu(c), w_ref[...], preferred_element_type=F32,
                         precision=lax.Precision.HIGHEST) + b_ref[...]


def _cond_call(c_pad, w_cond, b_cond):
    rows, d = c_pad.shape
    n = w_cond.shape[1]
    bn = 1024
    return pl.pallas_call(
        _cond_kernel,
        grid=(n // bn,),
        in_specs=[pl.BlockSpec((rows, d), lambda j: (0, 0)),
                  pl.BlockSpec((d, bn), lambda j: (0, j)),
                  pl.BlockSpec((1, bn), lambda j: (0, j))],
        out_specs=pl.BlockSpec((rows, bn), lambda j: (0, j)),
        out_shape=jax.ShapeDtypeStruct((rows, n), F32),
        name="cond",
    )(c_pad, w_cond, b_cond)


def _mixer_kernel(alpha, ts,
                  x_ref, cond_ref, wsc_ref, wz_ref, wxbc_ref, wdt_ref, scw_ref, xcw_ref, xcb_ref,
                  dtb_ref, alog_ref, dskip_ref, scnw_ref, ssnw_ref, wout_ref, lng_ref, lnb_ref,
                  g16_ref, e16_ref, e128_ref, tri_ref,
                  out_ref,
                  pbuf, xbuf, state, xdt_s, bc_s, adt_s, ybuf):
    s_idx = pl.program_id(1)

    @pl.when(s_idx == 0)
    def _():
        pbuf[0:CARRY, :] = jnp.zeros((CARRY, D_MODEL), F32)
        xbuf[0:CARRY, :] = jnp.zeros((CARRY, SSM_CONV_DIM), F32)
        state[...] = jnp.zeros(state.shape, F32)

    x = x_ref[0]
    cond = cond_ref[0]
    shift1 = cond[:, 0:D_MODEL]
    scale1 = cond[:, D_MODEL:2 * D_MODEL]
    gate1 = cond[:, 2 * D_MODEL:3 * D_MODEL]
    u1 = (x * (1.0 + scale1) + shift1).astype(BF16)

    sc = _dot(u1, wsc_ref[...])
    p = sc[:, D_MODEL:2 * D_MODEL] * sc[:, 2 * D_MODEL:3 * D_MODEL]
    pbuf[CARRY:CARRY + ts, :] = p
    conv = p * scw_ref[SC_WIN - 1:SC_WIN, :]
    for k in range(SC_WIN - 1):
        off = CARRY - (SC_WIN - 1) + k
        conv = conv + pbuf[off:off + ts, :] * scw_ref[k:k + 1, :]
    pbuf[0:CARRY, :] = pbuf[ts:ts + CARRY, :]
    ysc = sc[:, 0:D_MODEL] * conv
    ss = _dot((ysc * ysc).astype(BF16), g16_ref[...])
    r = lax.rsqrt(ss * (1.0 / HEAD_DIM) + RMS_EPS)
    ysc = ysc * _split_dot_lhs(r, e16_ref[...], 2) * scnw_ref[...]

    z = _dot(u1, wz_ref[...])
    xbuf[CARRY:CARRY + ts, :] = _dot(u1, wxbc_ref[...])
    xc = xcb_ref[...] + xbuf[CARRY:CARRY + ts, :] * xcw_ref[SSM_CONV - 1:SSM_CONV, :]
    for k in range(SSM_CONV - 1):
        off = CARRY - (SSM_CONV - 1) + k
        xc = xc + xbuf[off:off + ts, :] * xcw_ref[k:k + 1, :]
    xbuf[0:CARRY, :] = xbuf[ts:ts + CARRY, :]
    xc = _silu(xc)
    xs = xc[:, 0:D_MODEL]
    bc_s[...] = xc[:, D_MODEL:SSM_CONV_DIM]

    lane = lax.broadcasted_iota(jnp.int32, (1, LANES), 1)
    head_lane = lane < SSM_HEADS
    dt = jnp.where(head_lane, _softplus(_dot(u1, wdt_ref[...]) + dtb_ref[...]), 0.0)
    a_neg = jnp.where(head_lane, -jnp.exp(alog_ref[...]), 0.0)
    adt_s[...] = dt * a_neg
    xdt_s[...] = xs * _split_dot_lhs(dt, e16_ref[...], 2)

    row_i = lax.broadcasted_iota(jnp.int32, (SSM_CHUNK, SSM_CHUNK), 0)
    col_i = lax.broadcasted_iota(jnp.int32, (SSM_CHUNK, SSM_CHUNK), 1)
    causal = row_i >= col_i
    lane_p = lax.broadcasted_iota(jnp.int32, (SSM_CHUNK, LANES), 1)
    first_head = lane_p < HEAD_DIM

    def chunk_body(c, carry):
        r0 = pl.multiple_of(c * SSM_CHUNK, SSM_CHUNK)
        rows = pl.ds(r0, SSM_CHUNK)
        adt = adt_s[rows, :]
        acum = _split_dot_rhs(tri_ref[...], adt, 3)
        acum_t = acum.T
        acum_full = _split_dot_lhs(acum, e16_ref[...], 3)
        acum_b = _split_dot_lhs(acum, e128_ref[...], 3)
        alast_full = acum_full[SSM_CHUNK - 1:SSM_CHUNK, :]
        dfs_full = jnp.exp(acum_full)
        dte_full = jnp.exp(alast_full - acum_full)
        cdec_full = jnp.exp(alast_full)
        xdt = xdt_s[rows, :]
        for g in range(SSM_GROUPS):
            gl = slice(g * GROUP_W, (g + 1) * GROUP_W)
            b_g = bc_s[rows, g * SSM_STATE:(g + 1) * SSM_STATE].astype(BF16)
            c_g = bc_s[rows, SSM_GN + g * SSM_STATE:SSM_GN + (g + 1) * SSM_STATE].astype(BF16)
            cb = _dot_nt(c_g, b_g)
            st = state[g]
            y_off = _dot(c_g, st.astype(BF16)) * dfs_full[:, gl]
            xw = (xdt[:, gl] * dte_full[:, gl]).astype(BF16)
            state[g] = st * cdec_full[:, gl] + _dot_tn(b_g, xw)
            for pr in range(GROUP_W // LANES):
                h0 = g * (SSM_HEADS // SSM_GROUPS) + 2 * pr
                ms = []
                for h in (h0, h0 + 1):
                    seg = acum_b[:, h * LANES:(h + 1) * LANES] - acum_t[h:h + 1, :]
                    dec = jnp.where(causal, jnp.exp(jnp.minimum(seg, 0.0)), 0.0)
                    ms.append((cb * dec).astype(BF16))
                lhs = jnp.concatenate(ms, axis=1)
                pl_ = slice(g * GROUP_W + pr * LANES, g * GROUP_W + (pr + 1) * LANES)
                xp = xdt[:, pl_]
                rhs = jnp.concatenate([jnp.where(first_head, xp, 0.0),
                                       jnp.where(first_head, 0.0, xp)], axis=0).astype(BF16)
                y_pair = _dot(lhs, rhs) + y_off[:, pr * LANES:(pr + 1) * LANES]
                ybuf[rows, pl_] = y_pair
        return carry

    lax.fori_loop(0, ts // SSM_CHUNK, chunk_body, 0)

    y = ybuf[...] + xs * dskip_ref[...]
    y = y * _silu(z)
    parts = []
    for g in range(SSM_GROUPS):
        yg = y[:, g * GROUP_W:(g + 1) * GROUP_W]
        ms = jnp.mean(yg * yg, axis=-1, keepdims=True)
        parts.append(yg * lax.rsqrt(ms + RMS_EPS))
    y = jnp.concatenate(parts, axis=1) * ssnw_ref[...]

    ycat = jnp.concatenate([ysc.astype(BF16), y.astype(BF16)], axis=1)
    mix = _dot(ycat, wout_ref[...])
    out_ref[0] = _layer_norm(alpha * x + (1.0 + gate1) * mix, lng_ref[...], lnb_ref[...])


def _const_spec(shape):
    nd = len(shape)
    return pl.BlockSpec(shape, lambda b, s: (0,) * nd, pipeline_mode=pl.Buffered(1))


def _mixer_call(x, cond3, weights, alpha):
    bsz, seqlen, d = x.shape
    ts = MIX_TS
    kern = functools.partial(_mixer_kernel, alpha, ts)
    in_specs = [pl.BlockSpec((1, ts, d), lambda b, s: (b, s, 0)),
                pl.BlockSpec((1, 1, cond3.shape[2]), lambda b, s: (b, 0, 0))]
    in_specs += [_const_spec(w.shape) for w in weights]
    return pl.pallas_call(
        kern,
        grid=(bsz, seqlen // ts),
        in_specs=in_specs,
        out_specs=pl.BlockSpec((1, ts, d), lambda b, s: (b, s, 0)),
        out_shape=jax.ShapeDtypeStruct((bsz, seqlen, d), F32),
        scratch_shapes=[pltpu.VMEM((ts + CARRY, D_MODEL), F32),
                        pltpu.VMEM((ts + CARRY, SSM_CONV_DIM), F32),
                        pltpu.VMEM((SSM_GROUPS, SSM_STATE, GROUP_W), F32),
                        pltpu.VMEM((ts, D_MODEL), F32),
                        pltpu.VMEM((ts, 2 * SSM_GN), F32),
                        pltpu.VMEM((ts, LANES), F32),
                        pltpu.VMEM((ts, D_MODEL), F32)],
        compiler_params=pltpu.CompilerParams(
            dimension_semantics=("arbitrary", "arbitrary"), vmem_limit_bytes=VMEM_LIMIT),
        name="mixer",
    )(x, cond3, *weights)


def _top_values(s, count):
    vals = []
    cur = s
    for k in range(count):
        mx = jnp.max(cur, axis=0, keepdims=True)
        vals.append(mx)
        if k + 1 < count:
            cur = jnp.where(cur == mx, NEG_INF, cur)
    return vals


def _stack_rows(rows, row_idx):
    out = rows[-1]
    for r in range(len(rows) - 2, -1, -1):
        out = jnp.where(row_idx == r, rows[r], out)
    return out


def _prep_kernel(x1_ref, cond_ref, wqt_ref, k1_ref, k2_ref, u2t_ref, s2t_ref, a2_ref, c_ref):
    x1 = x1_ref[...]
    cond = cond_ref[0]
    shift2 = cond[:, 3 * D_MODEL:4 * D_MODEL]
    scale2 = cond[:, 4 * D_MODEL:5 * D_MODEL]
    u2t = (x1 * (1.0 + scale2) + shift2).T.astype(BF16)
    u2t_ref[...] = u2t
    q_t = _dot(wqt_ref[...], u2t)
    row16 = lax.broadcasted_iota(jnp.int32, (PEER_TOPK, 1), 0)
    for h in range(PEER_HEADS):
        q1 = q_t[h * PEER_DKEY:h * PEER_DKEY + PEER_HALF].astype(BF16)
        q2 = q_t[h * PEER_DKEY + PEER_HALF:(h + 1) * PEER_DKEY].astype(BF16)
        s1 = _dot(k1_ref[h], q1)
        s2 = _dot(k2_ref[h], q2)
        a = _top_values(s1, PEER_TOPK + 1)
        b = _top_values(s2, PEER_TOPK + 1)
        a_st = _stack_rows(a[:PEER_TOPK], row16)
        b_st = _stack_rows(b[:PEER_TOPK], row16)
        pieces = [a[r] + b_st for r in range(4)]
        pieces += [jnp.where(row16 >= 4, b[c] + a_st, NEG_INF) for c in range(3)]
        pieces.append(jnp.where(row16 == 0, a[PEER_TOPK] + b[0],
                                jnp.where(row16 == 1, a[0] + b[PEER_TOPK], NEG_INF)))
        cur = pieces
        tau = None
        for k in range(PEER_TOPK + 1):
            mx = cur[0]
            for pc in cur[1:]:
                mx = jnp.maximum(mx, pc)
            nxt = jnp.max(mx, axis=0, keepdims=True)
            if k + 1 == PEER_TOPK:
                tau = nxt
            if k < PEER_TOPK:
                cur = [jnp.where(pc == nxt, NEG_INF, pc) for pc in cur]
        cut = 0.5 * (tau + nxt)
        m = a[0] + b[0]
        zsum = None
        for pc in pieces:
            e = jnp.sum(jnp.where(pc >= tau, jnp.exp(pc - m), 0.0), axis=0, keepdims=True)
            zsum = e if zsum is None else zsum + e
        shift = m + jnp.log(zsum)
        s2t_ref[h] = s2
        a2_ref[h] = s1 - shift
        c_ref[h:h + 1, :] = cut - shift


def _prep_call(x1_flat, cond3, wqt, k1, k2, seqlen):
    t, d = x1_flat.shape
    tm = PREP_TM
    per_b = seqlen // tm
    return pl.pallas_call(
        _prep_kernel,
        grid=(t // tm,),
        in_specs=[pl.BlockSpec((tm, d), lambda i: (i, 0)),
                  pl.BlockSpec((1, 1, cond3.shape[2]), lambda i: (i // per_b, 0, 0)),
                  pl.BlockSpec(wqt.shape, lambda i: (0, 0), pipeline_mode=pl.Buffered(1)),
                  pl.BlockSpec(k1.shape, lambda i: (0, 0, 0), pipeline_mode=pl.Buffered(1)),
                  pl.BlockSpec(k2.shape, lambda i: (0, 0, 0), pipeline_mode=pl.Buffered(1))],
        out_specs=[pl.BlockSpec((d, tm), lambda i: (0, i)),
                   pl.BlockSpec((PEER_HEADS, PEER_NKEYS, tm), lambda i: (0, 0, i)),
                   pl.BlockSpec((PEER_HEADS, PEER_NKEYS, tm), lambda i: (0, 0, i)),
                   pl.BlockSpec((PEER_HEADS, tm), lambda i: (0, i))],
        out_shape=[jax.ShapeDtypeStruct((d, t), BF16),
                   jax.ShapeDtypeStruct((PEER_HEADS, PEER_NKEYS, t), F32),
                   jax.ShapeDtypeStruct((PEER_HEADS, PEER_NKEYS, t), F32),
                   jax.ShapeDtypeStruct((PEER_HEADS, t), F32)],
        compiler_params=pltpu.CompilerParams(
            dimension_semantics=("arbitrary",), vmem_limit_bytes=VMEM_LIMIT),
        name="peer_prep",
    )(x1_flat, cond3, wqt, k1, k2)


def _peer_kernel(alpha, n_i,
                 u2t_ref, s2t_ref, a2_ref, c_ref, u_ref, vt_ref, x1_ref, cond_ref, lng_ref, lnb_ref,
                 out_ref, acc_ref):
    e_idx = pl.program_id(1)

    @pl.when(e_idx == 0)
    def _():
        acc_ref[...] = jnp.zeros(acc_ref.shape, F32)

    h_t = _dot(u_ref[...], u2t_ref[...])
    acts = []
    for ii in range(n_i):
        i = e_idx * n_i + ii
        w = None
        for h in range(PEER_HEADS):
            xs = s2t_ref[h] + a2_ref[h, pl.ds(i, 1), :]
            wh = jnp.where(xs >= c_ref[h:h + 1, :], jnp.exp(xs), 0.0)
            w = wh if w is None else w + wh
        hh = h_t[ii * PEER_NKEYS:(ii + 1) * PEER_NKEYS]
        gelu = 0.5 * hh * (1.0 + lax.erf(hh * 0.7071067811865476))
        acts.append((gelu * w).astype(BF16))
    act_t = jnp.concatenate(acts, axis=0)
    acc_ref[...] += _dot(vt_ref[...], act_t)

    @pl.when(e_idx == pl.num_programs(1) - 1)
    def _():
        ffn = acc_ref[...].T
        gate2 = cond_ref[0][:, 5 * D_MODEL:6 * D_MODEL]
        out_ref[...] = _layer_norm(alpha * x1_ref[...] + (1.0 + gate2) * ffn,
                                   lng_ref[...], lnb_ref[...])


def _peer_call(u2t, s2t, a2, cthr, u_bf, vt_bf, x1_flat, cond3, ln_g, ln_b, seqlen, alpha):
    d, t = u2t.shape
    n_exp = u_bf.shape[0]
    tm, te = PEER_TM, PEER_TE
    per_b = seqlen // tm
    kern = functools.partial(_peer_kernel, alpha, te // PEER_NKEYS)
    return pl.pallas_call(
        kern,
        grid=(t // tm, n_exp // te),
        in_specs=[pl.BlockSpec((d, tm), lambda i, e: (0, i)),
                  pl.BlockSpec((PEER_HEADS, PEER_NKEYS, tm), lambda i, e: (0, 0, i)),
                  pl.BlockSpec((PEER_HEADS, PEER_NKEYS, tm), lambda i, e: (0, 0, i)),
                  pl.BlockSpec((PEER_HEADS, tm), lambda i, e: (0, i)),
                  pl.BlockSpec((te, d), lambda i, e: (e, 0)),
                  pl.BlockSpec((d, te), lambda i, e: (0, e)),
                  pl.BlockSpec((tm, d), lambda i, e: (i, 0)),
                  pl.BlockSpec((1, 1, cond3.shape[2]), lambda i, e: (i // per_b, 0, 0)),
                  pl.BlockSpec((1, d), lambda i, e: (0, 0)),
                  pl.BlockSpec((1, d), lambda i, e: (0, 0))],
        out_specs=pl.BlockSpec((tm, d), lambda i, e: (i, 0)),
        out_shape=jax.ShapeDtypeStruct((t, d), F32),
        scratch_shapes=[pltpu.VMEM((d, tm), F32)],
        compiler_params=pltpu.CompilerParams(
            dimension_semantics=("arbitrary", "arbitrary"), vmem_limit_bytes=VMEM_LIMIT),
        name="peer",
    )(u2t, s2t, a2, cthr, u_bf, vt_bf, x1_flat, cond3, ln_g, ln_b)


def _mixer_constants():
    ch = jnp.arange(D_MODEL)
    hd = jnp.arange(LANES)
    g16 = (ch[:, None] // HEAD_DIM == hd[None, :]).astype(BF16)
    e16 = (hd[:, None] == ch[None, :] // HEAD_DIM).astype(BF16)
    wide = jnp.arange(SSM_HEADS * LANES)
    e128 = (hd[:, None] == wide[None, :] // LANES).astype(BF16)
    tt = jnp.arange(SSM_CHUNK)
    tri = (tt[:, None] >= tt[None, :]).astype(BF16)
    return g16, e16, e128, tri


def _row(v):
    return v.reshape(1, -1).astype(F32)


def _pad_lanes(v):
    v = v.reshape(1, -1).astype(F32)
    return jnp.pad(v, ((0, 0), (0, LANES - v.shape[1])))


def _layer(x, cond3, alpha, w_in, sc_conv_w, ssm_conv_w, ssm_conv_b, dt_bias, a_log, d_skip,
           sc_norm_w, ssm_norm_w, w_out, ln1_g, ln1_b, w_query, sub_keys1, sub_keys2,
           expert_u, expert_v, ln2_g, ln2_b):
    bsz, seqlen, d = x.shape
    o_z = 3 * D_MODEL
    o_xbc = o_z + D_MODEL
    o_dt = o_xbc + SSM_CONV_DIM
    w_dt = jnp.pad(w_in[:, o_dt:], ((0, 0), (0, LANES - SSM_HEADS)))
    weights = [w_in[:, :o_z].astype(BF16), w_in[:, o_z:o_xbc].astype(BF16),
               w_in[:, o_xbc:o_dt].astype(BF16), w_dt.astype(BF16),
               sc_conv_w.astype(F32), ssm_conv_w.astype(F32), _row(ssm_conv_b),
               _pad_lanes(dt_bias), _pad_lanes(a_log), _row(jnp.repeat(d_skip, HEAD_DIM)),
               _row(sc_norm_w), _row(ssm_norm_w), w_out.astype(BF16), _row(ln1_g), _row(ln1_b),
               *_mixer_constants()]
    x1 = _mixer_call(x, cond3, weights, alpha)
    x1_flat = x1.reshape(bsz * seqlen, d)
    u2t, s2t, a2, cthr = _prep_call(x1_flat, cond3, w_query.T.astype(BF16),
                                    sub_keys1.astype(BF16), sub_keys2.astype(BF16), seqlen)
    out = _peer_call(u2t, s2t, a2, cthr, expert_u.astype(BF16), expert_v.T.astype(BF16),
                     x1_flat, cond3, _row(ln2_g), _row(ln2_b), seqlen, alpha)
    return out.reshape(bsz, seqlen, d)


def kernel(x, c, w_cond, b_cond, w_in, sc_conv_w, ssm_conv_w, ssm_conv_b, dt_bias, a_log, d_skip,
           sc_norm_w, ssm_norm_w, w_out, ln1_g, ln1_b, w_query, sub_keys1, sub_keys2, expert_u,
           expert_v, ln2_g, ln2_b):
    depth = w_cond.shape[0]
    bsz = x.shape[0]
    alpha = (2.0 * depth) ** 0.25
    c_pad = jnp.pad(c, ((0, (-bsz) % CARRY), (0, 0)))
    for l in range(depth):
        cond = _cond_call(c_pad, w_cond[l], b_cond[l].reshape(1, -1))[:bsz]
        cond3 = cond.reshape(bsz, 1, -1)
        x = _layer(x, cond3, alpha, w_in[l], sc_conv_w[l], ssm_conv_w[l], ssm_conv_b[l], dt_bias[l],
                   a_log[l], d_skip[l], sc_norm_w[l], ssm_norm_w[l], w_out[l], ln1_g[l], ln1_b[l],
                   w_query[l], sub_keys1[l], sub_keys2[l], expert_u[l], expert_v[l], ln2_g[l],
                   ln2_b[l])
    return x
```

```python
import functools

import jax
import jax.numpy as jnp
from jax import lax
from jax.experimental import pallas as pl
from jax.experimental.pallas import tpu as pltpu

F32 = jnp.float32
BF16 = jnp.bfloat16

D_MODEL = 1024
HEAD_DIM = 64
SC_HEADS = 16
SC_WIN = 3
SSM_HEADS = 16
SSM_GROUPS = 2
SSM_STATE = 128
SSM_CONV = 4
SSM_CHUNK = 128
SSM_GN = SSM_GROUPS * SSM_STATE
SSM_CONV_DIM = D_MODEL + 2 * SSM_GN
GROUP_W = D_MODEL // SSM_GROUPS
PEER_HEADS = 8
PEER_NKEYS = 128
PEER_TOPK = 16
PEER_HALF = 128
PEER_DKEY = 256
LN_EPS = 1e-5
RMS_EPS = 1e-6
LANES = 128
CARRY = 8
NEG_INF = float("-inf")

MIX_TS = 256
PREP_TM = 256
PEER_TM = 512
PEER_TE = 1024
PEER_SUB_I = 4
VMEM_LIMIT = 56 * 1024 * 1024


def _dot(a, b):
    return jnp.dot(a, b, preferred_element_type=F32)


def _dot_nt(a, b):
    return lax.dot_general(a, b, (((1,), (1,)), ((), ())), preferred_element_type=F32)


def _dot_tn(a, b):
    return lax.dot_general(a, b, (((0,), (0,)), ((), ())), preferred_element_type=F32)


def _split_dot_lhs(a, m, terms):
    out = None
    r = a
    for _ in range(terms):
        hi = r.astype(BF16)
        part = _dot(hi, m)
        out = part if out is None else out + part
        r = r - hi.astype(F32)
    return out


def _split_dot_rhs(m, a, terms):
    out = None
    r = a
    for _ in range(terms):
        hi = r.astype(BF16)
        part = _dot(m, hi)
        out = part if out is None else out + part
        r = r - hi.astype(F32)
    return out


def _sigmoid(x):
    return 1.0 / (1.0 + jnp.exp(-x))


def _silu(x):
    return x * _sigmoid(x)


def _softplus(x):
    return jnp.maximum(x, 0.0) + jnp.log(1.0 + jnp.exp(-jnp.abs(x)))


def _layer_norm(v, g, b):
    mu = jnp.mean(v, axis=-1, keepdims=True)
    d = v - mu
    var = jnp.mean(d * d, axis=-1, keepdims=True)
    return d * lax.rsqrt(var + LN_EPS) * g + b


def _cond_kernel(c_ref, w_ref, b_ref, o_ref):
    c = c_ref[...]
    o_ref[...] = jnp.dot(_silu(c), w_ref[...], preferred_element_type=F32,
                         precision=lax.Precision.HIGHEST) + b_ref[...]


def _cond_call(c_pad, w_cond, b_cond):
    rows, d = c_pad.shape
    n = w_cond.shape[1]
    bn = 1024
    return pl.pallas_call(
        _cond_kernel,
        grid=(n // bn,),
        in_specs=[pl.BlockSpec((rows, d), lambda j: (0, 0)),
                  pl.BlockSpec((d, bn), lambda j: (0, j)),
                  pl.BlockSpec((1, bn), lambda j: (0, j))],
        out_specs=pl.BlockSpec((rows, bn), lambda j: (0, j)),
        out_shape=jax.ShapeDtypeStruct((rows, n), F32),
        name="cond",
    )(c_pad, w_cond, b_cond)


def _mixer_kernel(alpha, ts,
                  x_ref, cond_ref, wsc_ref, wz_ref, wxbc_ref, wdt_ref, scw_ref, xcw_ref, xcb_ref,
                  dtb_ref, alog_ref, dskip_ref, scnw_ref, ssnw_ref, wout_ref, lng_ref, lnb_ref,
                  g16_ref, e16_ref, e128_ref, tri_ref,
                  out_ref,
                  pbuf, xbuf, state, xdt_s, bc_s, adt_s, ybuf):
    s_idx = pl.program_id(1)

    @pl.when(s_idx == 0)
    def _():
        pbuf[0:CARRY, :] = jnp.zeros((CARRY, D_MODEL), F32)
        xbuf[0:CARRY, :] = jnp.zeros((CARRY, SSM_CONV_DIM), F32)
        state[...] = jnp.zeros(state.shape, F32)

    x = x_ref[0]
    cond = cond_ref[0]
    shift1 = cond[:, 0:D_MODEL]
    scale1 = cond[:, D_MODEL:2 * D_MODEL]
    gate1 = cond[:, 2 * D_MODEL:3 * D_MODEL]
    u1 = (x * (1.0 + scale1) + shift1).astype(BF16)

    sc = _dot(u1, wsc_ref[...])
    p = sc[:, D_MODEL:2 * D_MODEL] * sc[:, 2 * D_MODEL:3 * D_MODEL]
    pbuf[CARRY:CARRY + ts, :] = p
    conv = p * scw_ref[SC_WIN - 1:SC_WIN, :]
    for k in range(SC_WIN - 1):
        off = CARRY - (SC_WIN - 1) + k
        conv = conv + pbuf[off:off + ts, :] * scw_ref[k:k + 1, :]
    pbuf[0:CARRY, :] = pbuf[ts:ts + CARRY, :]
    ysc = sc[:, 0:D_MODEL] * conv
    ss = _dot((ysc * ysc).astype(BF16), g16_ref[...])
    r = lax.rsqrt(ss * (1.0 / HEAD_DIM) + RMS_EPS)
    ysc = ysc * _split_dot_lhs(r, e16_ref[...], 2) * scnw_ref[...]

    z = _dot(u1, wz_ref[...])
    xbuf[CARRY:CARRY + ts, :] = _dot(u1, wxbc_ref[...])
    xc = xcb_ref[...] + xbuf[CARRY:CARRY + ts, :] * xcw_ref[SSM_CONV - 1:SSM_CONV, :]
    for k in range(SSM_CONV - 1):
        off = CARRY - (SSM_CONV - 1) + k
        xc = xc + xbuf[off:off + ts, :] * xcw_ref[k:k + 1, :]
    xbuf[0:CARRY, :] = xbuf[ts:ts + CARRY, :]
    xc = _silu(xc)
    xs = xc[:, 0:D_MODEL]
    bc_s[...] = xc[:, D_MODEL:SSM_CONV_DIM]

    lane = lax.broadcasted_iota(jnp.int32, (1, LANES), 1)
    head_lane = lane < SSM_HEADS
    dt = jnp.where(head_lane, _softplus(_dot(u1, wdt_ref[...]) + dtb_ref[...]), 0.0)
    a_neg = jnp.where(head_lane, -jnp.exp(alog_ref[...]), 0.0)
    adt_s[...] = dt * a_neg
    xdt_s[...] = xs * _split_dot_lhs(dt, e16_ref[...], 2)

    row_i = lax.broadcasted_iota(jnp.int32, (SSM_CHUNK, SSM_CHUNK), 0)
    col_i = lax.broadcasted_iota(jnp.int32, (SSM_CHUNK, SSM_CHUNK), 1)
    causal = row_i >= col_i
    lane_p = lax.broadcasted_iota(jnp.int32, (SSM_CHUNK, LANES), 1)
    first_head = lane_p < HEAD_DIM

    def chunk_body(c, carry):
        r0 = pl.multiple_of(c * SSM_CHUNK, SSM_CHUNK)
        rows = pl.ds(r0, SSM_CHUNK)
        adt = adt_s[rows, :]
        acum = _split_dot_rhs(tri_ref[...], adt, 3)
        acum_t = acum.T
        acum_full = _split_dot_lhs(acum, e16_ref[...], 3)
        acum_b = _split_dot_lhs(acum, e128_ref[...], 3)
        alast_full = acum_full[SSM_CHUNK - 1:SSM_CHUNK, :]
        dfs_full = jnp.exp(acum_full)
        dte_full = jnp.exp(alast_full - acum_full)
        cdec_full = jnp.exp(alast_full)
        xdt = xdt_s[rows, :]
        for g in range(SSM_GROUPS):
            gl = slice(g * GROUP_W, (g + 1) * GROUP_W)
            b_g = bc_s[rows, g * SSM_STATE:(g + 1) * SSM_STATE].astype(BF16)
            c_g = bc_s[rows, SSM_GN + g * SSM_STATE:SSM_GN + (g + 1) * SSM_STATE].astype(BF16)
            cb = _dot_nt(c_g, b_g)
            st = state[g]
            y_off = _dot(c_g, st.astype(BF16)) * dfs_full[:, gl]
            xw = (xdt[:, gl] * dte_full[:, gl]).astype(BF16)
            state[g] = st * cdec_full[:, gl] + _dot_tn(b_g, xw)
            for pr in range(GROUP_W // LANES):
                h0 = g * (SSM_HEADS // SSM_GROUPS) + 2 * pr
                ms = []
                for h in (h0, h0 + 1):
                    seg = acum_b[:, h * LANES:(h + 1) * LANES] - acum_t[h:h + 1, :]
                    dec = jnp.where(causal, jnp.exp(jnp.minimum(seg, 0.0)), 0.0)
                    ms.append((cb * dec).astype(BF16))
                lhs = jnp.concatenate(ms, axis=1)
                pl_ = slice(g * GROUP_W + pr * LANES, g * GROUP_W + (pr + 1) * LANES)
                xp = xdt[:, pl_]
                rhs = jnp.concatenate([jnp.where(first_head, xp, 0.0),
                                       jnp.where(first_head, 0.0, xp)], axis=0).astype(BF16)
                y_pair = _dot(lhs, rhs) + y_off[:, pr * LANES:(pr + 1) * LANES]
                ybuf[rows, pl_] = y_pair
        return carry

    lax.fori_loop(0, ts // SSM_CHUNK, chunk_body, 0)

    y = ybuf[...] + xs * dskip_ref[...]
    y = y * _silu(z)
    parts = []
    for g in range(SSM_GROUPS):
        yg = y[:, g * GROUP_W:(g + 1) * GROUP_W]
        ms = jnp.mean(yg * yg, axis=-1, keepdims=True)
        parts.append(yg * lax.rsqrt(ms + RMS_EPS))
    y = jnp.concatenate(parts, axis=1) * ssnw_ref[...]

    ycat = jnp.concatenate([ysc.astype(BF16), y.astype(BF16)], axis=1)
    mix = _dot(ycat, wout_ref[...])
    out_ref[0] = _layer_norm(alpha * x + (1.0 + gate1) * mix, lng_ref[...], lnb_ref[...])


def _const_spec(shape):
    nd = len(shape)
    return pl.BlockSpec(shape, lambda b, s: (0,) * nd, pipeline_mode=pl.Buffered(1))


def _mixer_call(x, cond3, weights, alpha):
    bsz, seqlen, d = x.shape
    ts = MIX_TS
    kern = functools.partial(_mixer_kernel, alpha, ts)
    in_specs = [pl.BlockSpec((1, ts, d), lambda b, s: (b, s, 0)),
                pl.BlockSpec((1, 1, cond3.shape[2]), lambda b, s: (b, 0, 0))]
    in_specs += [_const_spec(w.shape) for w in weights]
    return pl.pallas_call(
        kern,
        grid=(bsz, seqlen // ts),
        in_specs=in_specs,
        out_specs=pl.BlockSpec((1, ts, d), lambda b, s: (b, s, 0)),
        out_shape=jax.ShapeDtypeStruct((bsz, seqlen, d), F32),
        scratch_shapes=[pltpu.VMEM((ts + CARRY, D_MODEL), F32),
                        pltpu.VMEM((ts + CARRY, SSM_CONV_DIM), F32),
                        pltpu.VMEM((SSM_GROUPS, SSM_STATE, GROUP_W), F32),
                        pltpu.VMEM((ts, D_MODEL), F32),
                        pltpu.VMEM((ts, 2 * SSM_GN), F32),
                        pltpu.VMEM((ts, LANES), F32),
                        pltpu.VMEM((ts, D_MODEL), F32)],
        compiler_params=pltpu.CompilerParams(
            dimension_semantics=("arbitrary", "arbitrary"), vmem_limit_bytes=VMEM_LIMIT),
        name="mixer",
    )(x, cond3, *weights)


def _top_ranked(s, count):
    vals = []
    cur = s
    rank = jnp.full(s.shape, float(count), F32)
    for k in range(count):
        mx = jnp.max(cur, axis=0, keepdims=True)
        hit = cur == mx
        vals.append(mx)
        rank = jnp.where(hit, float(k), rank)
        if k + 1 < count:
            cur = jnp.where(hit, NEG_INF, cur)
    return vals, rank


def _stack_rows(rows, row_idx):
    out = rows[-1]
    for r in range(len(rows) - 2, -1, -1):
        out = jnp.where(row_idx == r, rows[r], out)
    return out


def _prep_kernel(x1_ref, cond_ref, wqt_ref, k1_ref, k2_ref, u2t_ref, rank2_ref, e2_ref, l1_ref, e1_ref):
    x1 = x1_ref[...]
    cond = cond_ref[0]
    shift2 = cond[:, 3 * D_MODEL:4 * D_MODEL]
    scale2 = cond[:, 4 * D_MODEL:5 * D_MODEL]
    u2t = (x1 * (1.0 + scale2) + shift2).T.astype(BF16)
    u2t_ref[...] = u2t
    q_t = _dot(wqt_ref[...], u2t)
    row16 = lax.broadcasted_iota(jnp.int32, (PEER_TOPK, 1), 0)
    for h in range(PEER_HEADS):
        q1 = q_t[h * PEER_DKEY:h * PEER_DKEY + PEER_HALF].astype(BF16)
        q2 = q_t[h * PEER_DKEY + PEER_HALF:(h + 1) * PEER_DKEY].astype(BF16)
        s1 = _dot(k1_ref[h], q1)
        s2 = _dot(k2_ref[h], q2)
        a, rank1 = _top_ranked(s1, PEER_TOPK)
        b, rank2 = _top_ranked(s2, PEER_TOPK)
        a_st = _stack_rows(a, row16)
        b_st = _stack_rows(b, row16)
        pieces = [a[r] + b_st for r in range(4)]
        pieces += [jnp.where(row16 >= 4, b[c] + a_st, NEG_INF) for c in range(3)]
        cur = pieces
        tau = None
        for k in range(PEER_TOPK):
            mx = cur[0]
            for pc in cur[1:]:
                mx = jnp.maximum(mx, pc)
            tau = jnp.max(mx, axis=0, keepdims=True)
            if k + 1 < PEER_TOPK:
                cur = [jnp.where(pc == tau, NEG_INF, pc) for pc in cur]
        m = a[0] + b[0]
        zsum = None
        for pc in pieces:
            e = jnp.sum(jnp.where(pc >= tau, jnp.exp(pc - m), 0.0), axis=0, keepdims=True)
            zsum = e if zsum is None else zsum + e
        lmap = jnp.zeros(s1.shape, F32)
        for r in range(PEER_TOPK):
            cnt = jnp.sum(jnp.where(a[r] + b_st >= tau, 1.0, 0.0), axis=0, keepdims=True)
            lmap = jnp.where(rank1 == float(r), cnt, lmap)
        rank2_ref[h] = rank2.astype(BF16)
        e2_ref[h] = jnp.exp(s2 - b[0]).astype(BF16)
        l1_ref[h] = lmap
        e1_ref[h] = jnp.exp(s1 - a[0]) * (1.0 / zsum)


def _prep_call(x1_flat, cond3, wqt, k1, k2, seqlen):
    t, d = x1_flat.shape
    tm = PREP_TM
    per_b = seqlen // tm
    score_spec = pl.BlockSpec((PEER_HEADS, PEER_NKEYS, tm), lambda i: (0, 0, i))
    return pl.pallas_call(
        _prep_kernel,
        grid=(t // tm,),
        in_specs=[pl.BlockSpec((tm, d), lambda i: (i, 0)),
                  pl.BlockSpec((1, 1, cond3.shape[2]), lambda i: (i // per_b, 0, 0)),
                  pl.BlockSpec(wqt.shape, lambda i: (0, 0), pipeline_mode=pl.Buffered(1)),
                  pl.BlockSpec(k1.shape, lambda i: (0, 0, 0), pipeline_mode=pl.Buffered(1)),
                  pl.BlockSpec(k2.shape, lambda i: (0, 0, 0), pipeline_mode=pl.Buffered(1))],
        out_specs=[pl.BlockSpec((d, tm), lambda i: (0, i)),
                   score_spec, score_spec, score_spec, score_spec],
        out_shape=[jax.ShapeDtypeStruct((d, t), BF16),
                   jax.ShapeDtypeStruct((PEER_HEADS, PEER_NKEYS, t), BF16),
                   jax.ShapeDtypeStruct((PEER_HEADS, PEER_NKEYS, t), BF16),
                   jax.ShapeDtypeStruct((PEER_HEADS, PEER_NKEYS, t), F32),
                   jax.ShapeDtypeStruct((PEER_HEADS, PEER_NKEYS, t), F32)],
        compiler_params=pltpu.CompilerParams(
            dimension_semantics=("arbitrary",), vmem_limit_bytes=VMEM_LIMIT),
        name="peer_prep",
    )(x1_flat, cond3, wqt, k1, k2)


def _peer_kernel(alpha, n_sub, sub_i,
                 u2t_ref, rank2_ref, e2_ref, l1_ref, e1_ref, u_ref, vt_ref, x1_ref, cond_ref,
                 lng_ref, lnb_ref, out_ref, acc_ref, h_ref, act_ref):
    e_idx = pl.program_id(1)

    @pl.when(e_idx == 0)
    def _():
        acc_ref[...] = jnp.zeros(acc_ref.shape, F32)

    sub = sub_i * PEER_NKEYS
    tm = u2t_ref.shape[1]
    for sb in range(n_sub):
        rows = slice(sb * sub, (sb + 1) * sub)
        h_ref[rows, :] = _dot(u_ref[rows, :], u2t_ref[...])
    for blk in range(n_sub * sub_i):
        i = e_idx * (n_sub * sub_i) + blk
        rows = slice(blk * PEER_NKEYS, (blk + 1) * PEER_NKEYS)
        w = None
        for h in range(PEER_HEADS):
            lrow = l1_ref[h, pl.ds(i, 1), :].astype(BF16)
            erow = e1_ref[h, pl.ds(i, 1), :].astype(BF16)
            wh = jnp.where(rank2_ref[h] < lrow, e2_ref[h], 0.0) * erow
            w = wh if w is None else w + wh
        hh = h_ref[rows, :]
        gelu = 0.5 * hh * (1.0 + lax.erf(hh * 0.7071067811865476))
        act_ref[rows, :] = gelu.astype(BF16) * w
    total = None
    for sb in range(n_sub):
        rows = slice(sb * sub, (sb + 1) * sub)
        part = _dot(vt_ref[:, rows], act_ref[rows, :])
        total = part if total is None else total + part
    acc_ref[...] += total

    @pl.when(e_idx == pl.num_programs(1) - 1)
    def _():
        ffn = acc_ref[...].T
        gate2 = cond_ref[0][:, 5 * D_MODEL:6 * D_MODEL]
        out_ref[...] = _layer_norm(alpha * x1_ref[...] + (1.0 + gate2) * ffn,
                                   lng_ref[...], lnb_ref[...])


def _peer_call(u2t, rank2, e2, l1, e1, u_bf, vt_bf, x1_flat, cond3, ln_g, ln_b, seqlen, alpha):
    d, t = u2t.shape
    n_exp = u_bf.shape[0]
    tm, te = PEER_TM, PEER_TE
    per_b = seqlen // tm
    kern = functools.partial(_peer_kernel, alpha, te // (PEER_SUB_I * PEER_NKEYS), PEER_SUB_I)
    score_spec = pl.BlockSpec((PEER_HEADS, PEER_NKEYS, tm), lambda i, e: (0, 0, i))
    return pl.pallas_call(
        kern,
        grid=(t // tm, n_exp // te),
        in_specs=[pl.BlockSpec((d, tm), lambda i, e: (0, i)),
                  score_spec, score_spec, score_spec, score_spec,
                  pl.BlockSpec((te, d), lambda i, e: (e, 0)),
                  pl.BlockSpec((d, te), lambda i, e: (0, e)),
                  pl.BlockSpec((tm, d), lambda i, e: (i, 0)),
                  pl.BlockSpec((1, 1, cond3.shape[2]), lambda i, e: (i // per_b, 0, 0)),
                  pl.BlockSpec((1, d), lambda i, e: (0, 0)),
                  pl.BlockSpec((1, d), lambda i, e: (0, 0))],
        out_specs=pl.BlockSpec((tm, d), lambda i, e: (i, 0)),
        out_shape=jax.ShapeDtypeStruct((t, d), F32),
        scratch_shapes=[pltpu.VMEM((d, tm), F32), pltpu.VMEM((te, tm), F32),
                        pltpu.VMEM((te, tm), BF16)],
        compiler_params=pltpu.CompilerParams(
            dimension_semantics=("arbitrary", "arbitrary"), vmem_limit_bytes=VMEM_LIMIT),
        name="peer",
    )(u2t, rank2, e2, l1, e1, u_bf, vt_bf, x1_flat, cond3, ln_g, ln_b)


def _mixer_constants():
    ch = jnp.arange(D_MODEL)
    hd = jnp.arange(LANES)
    g16 = (ch[:, None] // HEAD_DIM == hd[None, :]).astype(BF16)
    e16 = (hd[:, None] == ch[None, :] // HEAD_DIM).astype(BF16)
    wide = jnp.arange(SSM_HEADS * LANES)
    e128 = (hd[:, None] == wide[None, :] // LANES).astype(BF16)
    tt = jnp.arange(SSM_CHUNK)
    tri = (tt[:, None] >= tt[None, :]).astype(BF16)
    return g16, e16, e128, tri


def _row(v):
    return v.reshape(1, -1).astype(F32)


def _pad_lanes(v):
    v = v.reshape(1, -1).astype(F32)
    return jnp.pad(v, ((0, 0), (0, LANES - v.shape[1])))


def _layer(x, cond3, alpha, w_in, sc_conv_w, ssm_conv_w, ssm_conv_b, dt_bias, a_log, d_skip,
           sc_norm_w, ssm_norm_w, w_out, ln1_g, ln1_b, w_query, sub_keys1, sub_keys2,
           expert_u, expert_v, ln2_g, ln2_b):
    bsz, seqlen, d = x.shape
    o_z = 3 * D_MODEL
    o_xbc = o_z + D_MODEL
    o_dt = o_xbc + SSM_CONV_DIM
    w_dt = jnp.pad(w_in[:, o_dt:], ((0, 0), (0, LANES - SSM_HEADS)))
    weights = [w_in[:, :o_z].astype(BF16), w_in[:, o_z:o_xbc].astype(BF16),
               w_in[:, o_xbc:o_dt].astype(BF16), w_dt.astype(BF16),
               sc_conv_w.astype(F32), ssm_conv_w.astype(F32), _row(ssm_conv_b),
               _pad_lanes(dt_bias), _pad_lanes(a_log), _row(jnp.repeat(d_skip, HEAD_DIM)),
               _row(sc_norm_w), _row(ssm_norm_w), w_out.astype(BF16), _row(ln1_g), _row(ln1_b),
               *_mixer_constants()]
    x1 = _mixer_call(x, cond3, weights, alpha)
    x1_flat = x1.reshape(bsz * seqlen, d)
    u2t, rank2, e2, l1, e1 = _prep_call(x1_flat, cond3, w_query.T.astype(BF16),
                                        sub_keys1.astype(BF16), sub_keys2.astype(BF16), seqlen)
    out = _peer_call(u2t, rank2, e2, l1, e1, expert_u.astype(BF16), expert_v.T.astype(BF16),
                     x1_flat, cond3, _row(ln2_g), _row(ln2_b), seqlen, alpha)
    return out.reshape(bsz, seqlen, d)


def kernel(x, c, w_cond, b_cond, w_in, sc_conv_w, ssm_conv_w, ssm_conv_b, dt_bias, a_log, d_skip,
           sc_norm_w, ssm_norm_w, w_out, ln1_g, ln1_b, w_query, sub_keys1, sub_keys2, expert_u,
           expert_v, ln2_g, ln2_b):
    depth = w_cond.shape[0]
    bsz = x.shape[0]
    alpha = (2.0 * depth) ** 0.25
    c_pad = jnp.pad(c, ((0, (-bsz) % CARRY), (0, 0)))
    for l in range(depth):
        cond = _cond_call(c_pad, w_cond[l], b_cond[l].reshape(1, -1))[:bsz]
        cond3 = cond.reshape(bsz, 1, -1)
        x = _layer(x, cond3, alpha, w_in[l], sc_conv_w[l], ssm_conv_w[l], ssm_conv_b[l], dt_bias[l],
                   a_log[l], d_skip[l], sc_norm_w[l], ssm_norm_w[l], w_out[l], ln1_g[l], ln1_b[l],
                   w_query[l], sub_keys1[l], sub_keys2[l], expert_u[l], expert_v[l], ln2_g[l],
                   ln2_b[l])
    return x
```

```python
import functools

import jax
import jax.numpy as jnp
from jax import lax
from jax.experimental import pallas as pl
from jax.experimental.pallas import tpu as pltpu

F32 = jnp.float32
BF16 = jnp.bfloat16

D_MODEL = 1024
HEAD_DIM = 64
SC_HEADS = 16
SC_WIN = 3
SSM_HEADS = 16
SSM_GROUPS = 2
SSM_STATE = 128
SSM_CONV = 4
SSM_CHUNK = 128
SSM_GN = SSM_GROUPS * SSM_STATE
SSM_CONV_DIM = D_MODEL + 2 * SSM_GN
GROUP_W = D_MODEL // SSM_GROUPS
PEER_HEADS = 8
PEER_NKEYS = 128
PEER_TOPK = 16
PEER_HALF = 128
PEER_DKEY = 256
LN_EPS = 1e-5
RMS_EPS = 1e-6
LANES = 128
CARRY = 8
NEG_INF = float("-inf")

MIX_TS = 256
PREP_TM = 256
PEER_TM = 512
PEER_TE = 2048
PEER_SUB_I = 2
VMEM_LIMIT = 56 * 1024 * 1024


def _dot(a, b):
    return jnp.dot(a, b, preferred_element_type=F32)


def _dot_nt(a, b):
    return lax.dot_general(a, b, (((1,), (1,)), ((), ())), preferred_element_type=F32)


def _dot_tn(a, b):
    return lax.dot_general(a, b, (((0,), (0,)), ((), ())), preferred_element_type=F32)


def _split_dot_lhs(a, m, terms):
    out = None
    r = a
    for _ in range(terms):
        hi = r.astype(BF16)
        part = _dot(hi, m)
        out = part if out is None else out + part
        r = r - hi.astype(F32)
    return out


def _split_dot_rhs(m, a, terms):
    out = None
    r = a
    for _ in range(terms):
        hi = r.astype(BF16)
        part = _dot(m, hi)
        out = part if out is None else out + part
        r = r - hi.astype(F32)
    return out


def _sigmoid(x):
    return 1.0 / (1.0 + jnp.exp(-x))


def _silu(x):
    return x * _sigmoid(x)


def _softplus(x):
    return jnp.maximum(x, 0.0) + jnp.log(1.0 + jnp.exp(-jnp.abs(x)))


def _layer_norm(v, g, b):
    mu = jnp.mean(v, axis=-1, keepdims=True)
    d = v - mu
    var = jnp.mean(d * d, axis=-1, keepdims=True)
    return d * lax.rsqrt(var + LN_EPS) * g + b


def _cond_kernel(c_ref, w_ref, b_ref, o_ref):
    c = c_ref[...]
    o_ref[...] = jnp.dot(_silu(c), w_ref[...], preferred_element_type=F32,
                         precision=lax.Precision.HIGHEST) + b_ref[...]


def _cond_call(c_pad, w_cond, b_cond):
    rows, d = c_pad.shape
    n = w_cond.shape[1]
    bn = 1024
    return pl.pallas_call(
        _cond_kernel,
        grid=(n // bn,),
        in_specs=[pl.BlockSpec((rows, d), lambda j: (0, 0)),
                  pl.BlockSpec((d, bn), lambda j: (0, j)),
                  pl.BlockSpec((1, bn), lambda j: (0, j))],
        out_specs=pl.BlockSpec((rows, bn), lambda j: (0, j)),
        out_shape=jax.ShapeDtypeStruct((rows, n), F32),
        name="cond",
    )(c_pad, w_cond, b_cond)


def _mixer_kernel(alpha, ts,
                  x_ref, cond_ref, wsc_ref, wz_ref, wxbc_ref, wdt_ref, scw_ref, xcw_ref, xcb_ref,
                  dtb_ref, alog_ref, dskip_ref, scnw_ref, ssnw_ref, wout_ref, lng_ref, lnb_ref,
                  g16_ref, e16_ref, e128_ref, tri_ref,
                  out_ref,
                  pbuf, xbuf, state, xdt_s, bc_s, adt_s, ybuf):
    s_idx = pl.program_id(1)

    @pl.when(s_idx == 0)
    def _():
        pbuf[0:CARRY, :] = jnp.zeros((CARRY, D_MODEL), F32)
        xbuf[0:CARRY, :] = jnp.zeros((CARRY, SSM_CONV_DIM), F32)
        state[...] = jnp.zeros(state.shape, F32)

    x = x_ref[0]
    cond = cond_ref[0]
    shift1 = cond[:, 0:D_MODEL]
    scale1 = cond[:, D_MODEL:2 * D_MODEL]
    gate1 = cond[:, 2 * D_MODEL:3 * D_MODEL]
    u1 = (x * (1.0 + scale1) + shift1).astype(BF16)

    sc = _dot(u1, wsc_ref[...])
    p = sc[:, D_MODEL:2 * D_MODEL] * sc[:, 2 * D_MODEL:3 * D_MODEL]
    pbuf[CARRY:CARRY + ts, :] = p
    conv = p * scw_ref[SC_WIN - 1:SC_WIN, :]
    for k in range(SC_WIN - 1):
        off = CARRY - (SC_WIN - 1) + k
        conv = conv + pbuf[off:off + ts, :] * scw_ref[k:k + 1, :]
    pbuf[0:CARRY, :] = pbuf[ts:ts + CARRY, :]
    ysc = sc[:, 0:D_MODEL] * conv
    ss = _dot((ysc * ysc).astype(BF16), g16_ref[...])
    r = lax.rsqrt(ss * (1.0 / HEAD_DIM) + RMS_EPS)
    ysc = ysc * _split_dot_lhs(r, e16_ref[...], 2) * scnw_ref[...]

    z = _dot(u1, wz_ref[...])
    xbuf[CARRY:CARRY + ts, :] = _dot(u1, wxbc_ref[...])
    xc = xcb_ref[...] + xbuf[CARRY:CARRY + ts, :] * xcw_ref[SSM_CONV - 1:SSM_CONV, :]
    for k in range(SSM_CONV - 1):
        off = CARRY - (SSM_CONV - 1) + k
        xc = xc + xbuf[off:off + ts, :] * xcw_ref[k:k + 1, :]
    xbuf[0:CARRY, :] = xbuf[ts:ts + CARRY, :]
    xc = _silu(xc)
    xs = xc[:, 0:D_MODEL]
    bc_s[...] = xc[:, D_MODEL:SSM_CONV_DIM]

    lane = lax.broadcasted_iota(jnp.int32, (1, LANES), 1)
    head_lane = lane < SSM_HEADS
    dt = jnp.where(head_lane, _softplus(_dot(u1, wdt_ref[...]) + dtb_ref[...]), 0.0)
    a_neg = jnp.where(head_lane, -jnp.exp(alog_ref[...]), 0.0)
    adt_s[...] = dt * a_neg
    xdt_s[...] = xs * _split_dot_lhs(dt, e16_ref[...], 2)

    row_i = lax.broadcasted_iota(jnp.int32, (SSM_CHUNK, SSM_CHUNK), 0)
    col_i = lax.broadcasted_iota(jnp.int32, (SSM_CHUNK, SSM_CHUNK), 1)
    causal = row_i >= col_i
    lane_p = lax.broadcasted_iota(jnp.int32, (SSM_CHUNK, LANES), 1)
    first_head = lane_p < HEAD_DIM

    def chunk_body(c, carry):
        r0 = pl.multiple_of(c * SSM_CHUNK, SSM_CHUNK)
        rows = pl.ds(r0, SSM_CHUNK)
        adt = adt_s[rows, :]
        acum = _split_dot_rhs(tri_ref[...], adt, 3)
        acum_t = acum.T
        acum_full = _split_dot_lhs(acum, e16_ref[...], 2)
        acum_b = _split_dot_lhs(acum, e128_ref[...], 2)
        alast_full = acum_full[SSM_CHUNK - 1:SSM_CHUNK, :]
        dfs_full = jnp.exp(acum_full)
        dte_full = jnp.exp(alast_full - acum_full)
        cdec_full = jnp.exp(alast_full)
        xdt = xdt_s[rows, :]
        for g in range(SSM_GROUPS):
            gl = slice(g * GROUP_W, (g + 1) * GROUP_W)
            b_g = bc_s[rows, g * SSM_STATE:(g + 1) * SSM_STATE].astype(BF16)
            c_g = bc_s[rows, SSM_GN + g * SSM_STATE:SSM_GN + (g + 1) * SSM_STATE].astype(BF16)
            cb = _dot_nt(c_g, b_g)
            st = state[g]
            y_off = _dot(c_g, st.astype(BF16)) * dfs_full[:, gl]
            xw = (xdt[:, gl] * dte_full[:, gl]).astype(BF16)
            state[g] = st * cdec_full[:, gl] + _dot_tn(b_g, xw)
            for pr in range(GROUP_W // LANES):
                h0 = g * (SSM_HEADS // SSM_GROUPS) + 2 * pr
                ms = []
                for h in (h0, h0 + 1):
                    seg = acum_b[:, h * LANES:(h + 1) * LANES] - acum_t[h:h + 1, :]
                    dec = jnp.where(causal, jnp.exp(jnp.minimum(seg, 0.0)), 0.0)
                    ms.append((cb * dec).astype(BF16))
                lhs = jnp.concatenate(ms, axis=1)
                pl_ = slice(g * GROUP_W + pr * LANES, g * GROUP_W + (pr + 1) * LANES)
                xp = xdt[:, pl_]
                rhs = jnp.concatenate([jnp.where(first_head, xp, 0.0),
                                       jnp.where(first_head, 0.0, xp)], axis=0).astype(BF16)
                y_pair = _dot(lhs, rhs) + y_off[:, pr * LANES:(pr + 1) * LANES]
                ybuf[rows, pl_] = y_pair
        return carry

    lax.fori_loop(0, ts // SSM_CHUNK, chunk_body, 0)

    y = ybuf[...] + xs * dskip_ref[...]
    y = y * _silu(z)
    parts = []
    for g in range(SSM_GROUPS):
        yg = y[:, g * GROUP_W:(g + 1) * GROUP_W]
        ms = jnp.mean(yg * yg, axis=-1, keepdims=True)
        parts.append(yg * lax.rsqrt(ms + RMS_EPS))
    y = jnp.concatenate(parts, axis=1) * ssnw_ref[...]

    ycat = jnp.concatenate([ysc.astype(BF16), y.astype(BF16)], axis=1)
    mix = _dot(ycat, wout_ref[...])
    out_ref[0] = _layer_norm(alpha * x + (1.0 + gate1) * mix, lng_ref[...], lnb_ref[...])


def _const_spec(shape):
    nd = len(shape)
    return pl.BlockSpec(shape, lambda b, s: (0,) * nd, pipeline_mode=pl.Buffered(1))


def _mixer_call(x, cond3, weights, alpha):
    bsz, seqlen, d = x.shape
    ts = MIX_TS
    kern = functools.partial(_mixer_kernel, alpha, ts)
    in_specs = [pl.BlockSpec((1, ts, d), lambda b, s: (b, s, 0)),
                pl.BlockSpec((1, 1, cond3.shape[2]), lambda b, s: (b, 0, 0))]
    in_specs += [_const_spec(w.shape) for w in weights]
    return pl.pallas_call(
        kern,
        grid=(bsz, seqlen // ts),
        in_specs=in_specs,
        out_specs=pl.BlockSpec((1, ts, d), lambda b, s: (b, s, 0)),
        out_shape=jax.ShapeDtypeStruct((bsz, seqlen, d), F32),
        scratch_shapes=[pltpu.VMEM((ts + CARRY, D_MODEL), F32),
                        pltpu.VMEM((ts + CARRY, SSM_CONV_DIM), F32),
                        pltpu.VMEM((SSM_GROUPS, SSM_STATE, GROUP_W), F32),
                        pltpu.VMEM((ts, D_MODEL), F32),
                        pltpu.VMEM((ts, 2 * SSM_GN), F32),
                        pltpu.VMEM((ts, LANES), F32),
                        pltpu.VMEM((ts, D_MODEL), F32)],
        compiler_params=pltpu.CompilerParams(
            dimension_semantics=("arbitrary", "arbitrary"), vmem_limit_bytes=VMEM_LIMIT),
        name="mixer",
    )(x, cond3, *weights)


def _top_ranked(s, count, with_rank):
    vals = []
    cur = s
    rank = jnp.full(s.shape, float(count), F32) if with_rank else None
    for k in range(count):
        mx = jnp.max(cur, axis=0, keepdims=True)
        hit = cur == mx
        vals.append(mx)
        if with_rank:
            rank = jnp.where(hit, float(k), rank)
        if k + 1 < count:
            cur = jnp.where(hit, NEG_INF, cur)
    return vals, rank


def _stack_rows(rows, row_idx):
    out = rows[-1]
    for r in range(len(rows) - 2, -1, -1):
        out = jnp.where(row_idx == r, rows[r], out)
    return out


def _prep_kernel(x1_ref, cond_ref, wqt_ref, k1_ref, k2_ref, u2t_ref, rank2_ref, e2_ref, l1_ref, e1_ref):
    x1 = x1_ref[...]
    cond = cond_ref[0]
    shift2 = cond[:, 3 * D_MODEL:4 * D_MODEL]
    scale2 = cond[:, 4 * D_MODEL:5 * D_MODEL]
    u2t = (x1 * (1.0 + scale2) + shift2).T.astype(BF16)
    u2t_ref[...] = u2t
    q_t = _dot(wqt_ref[...], u2t)
    row16 = lax.broadcasted_iota(jnp.int32, (PEER_TOPK, 1), 0)
    for h in range(PEER_HEADS):
        q1 = q_t[h * PEER_DKEY:h * PEER_DKEY + PEER_HALF].astype(BF16)
        q2 = q_t[h * PEER_DKEY + PEER_HALF:(h + 1) * PEER_DKEY].astype(BF16)
        s1 = _dot(k1_ref[h], q1)
        s2 = _dot(k2_ref[h], q2)
        a, _ = _top_ranked(s1, PEER_TOPK, False)
        b, rank2 = _top_ranked(s2, PEER_TOPK, True)
        a_st = _stack_rows(a, row16)
        b_st = _stack_rows(b, row16)
        pieces = [a[r] + b_st for r in range(4)]
        pieces += [jnp.where(row16 >= 4, b[c] + a_st, NEG_INF) for c in range(3)]
        cur = pieces
        tau = None
        for k in range(PEER_TOPK):
            mx = cur[0]
            for pc in cur[1:]:
                mx = jnp.maximum(mx, pc)
            tau = jnp.max(mx, axis=0, keepdims=True)
            if k + 1 < PEER_TOPK:
                cur = [jnp.where(pc == tau, NEG_INF, pc) for pc in cur]
        m = a[0] + b[0]
        zsum = None
        for pc in pieces:
            e = jnp.sum(jnp.where(pc >= tau, jnp.exp(pc - m), 0.0), axis=0, keepdims=True)
            zsum = e if zsum is None else zsum + e
        lmap = jnp.zeros(s1.shape, F32)
        for r in range(PEER_TOPK):
            cnt = jnp.sum(jnp.where(a[r] + b_st >= tau, 1.0, 0.0), axis=0, keepdims=True)
            lmap = jnp.where(s1 == a[r], cnt, lmap)
        rank2_ref[h] = rank2.astype(BF16)
        e2_ref[h] = jnp.exp(s2 - b[0]).astype(BF16)
        l1_ref[h] = lmap
        e1_ref[h] = jnp.exp(s1 - a[0]) * (0.5 / zsum)


def _prep_call(x1_flat, cond3, wqt, k1, k2, seqlen):
    t, d = x1_flat.shape
    tm = PREP_TM
    per_b = seqlen // tm
    score_spec = pl.BlockSpec((PEER_HEADS, PEER_NKEYS, tm), lambda i: (0, 0, i))
    return pl.pallas_call(
        _prep_kernel,
        grid=(t // tm,),
        in_specs=[pl.BlockSpec((tm, d), lambda i: (i, 0)),
                  pl.BlockSpec((1, 1, cond3.shape[2]), lambda i: (i // per_b, 0, 0)),
                  pl.BlockSpec(wqt.shape, lambda i: (0, 0), pipeline_mode=pl.Buffered(1)),
                  pl.BlockSpec(k1.shape, lambda i: (0, 0, 0), pipeline_mode=pl.Buffered(1)),
                  pl.BlockSpec(k2.shape, lambda i: (0, 0, 0), pipeline_mode=pl.Buffered(1))],
        out_specs=[pl.BlockSpec((d, tm), lambda i: (0, i)),
                   score_spec, score_spec, score_spec, score_spec],
        out_shape=[jax.ShapeDtypeStruct((d, t), BF16),
                   jax.ShapeDtypeStruct((PEER_HEADS, PEER_NKEYS, t), BF16),
                   jax.ShapeDtypeStruct((PEER_HEADS, PEER_NKEYS, t), BF16),
                   jax.ShapeDtypeStruct((PEER_HEADS, PEER_NKEYS, t), F32),
                   jax.ShapeDtypeStruct((PEER_HEADS, PEER_NKEYS, t), F32)],
        compiler_params=pltpu.CompilerParams(
            dimension_semantics=("arbitrary",), vmem_limit_bytes=VMEM_LIMIT),
        name="peer_prep",
    )(x1_flat, cond3, wqt, k1, k2)


def _peer_kernel(alpha, n_sub, sub_i,
                 u2t_ref, rank2_ref, e2_ref, l1_ref, e1_ref, u_ref, vt_ref, x1_ref,
                 cond_ref, lng_ref, lnb_ref, out_ref,
                 acc_ref, h_even, h_odd, w_even, w_odd, act_even, act_odd):
    e_idx = pl.program_id(1)

    @pl.when(e_idx == 0)
    def _():
        acc_ref[...] = jnp.zeros(acc_ref.shape, F32)

    sub = sub_i * PEER_NKEYS
    h_bufs, w_bufs, act_bufs = (h_even, h_odd), (w_even, w_odd), (act_even, act_odd)

    def first_matmul(k):
        h_bufs[k % 2][...] = _dot(u_ref[k * sub:(k + 1) * sub, :], u2t_ref[...])

    def gates(k):
        for ii in range(sub_i):
            i = e_idx * (n_sub * sub_i) + k * sub_i + ii
            w = None
            for h in range(PEER_HEADS):
                lrow = l1_ref[h, pl.ds(i, 1), :].astype(BF16)
                erow = e1_ref[h, pl.ds(i, 1), :].astype(BF16)
                wh = jnp.where(rank2_ref[h] < lrow, e2_ref[h], 0.0) * erow
                w = wh if w is None else w + wh
            w_bufs[k % 2][ii * PEER_NKEYS:(ii + 1) * PEER_NKEYS, :] = w

    def activate(k):
        hh = h_bufs[k % 2][...]
        gelu2 = hh * (1.0 + lax.erf(hh * 0.7071067811865476))
        act_bufs[k % 2][...] = gelu2.astype(BF16) * w_bufs[k % 2][...]

    def second_matmul(k):
        acc_ref[...] += _dot(vt_ref[k], act_bufs[k % 2][...])

    first_matmul(0)
    gates(0)
    for k in range(n_sub):
        if k + 1 < n_sub:
            first_matmul(k + 1)
            gates(k + 1)
        activate(k)
        if k >= 1:
            second_matmul(k - 1)
    second_matmul(n_sub - 1)

    @pl.when(e_idx == pl.num_programs(1) - 1)
    def _():
        ffn = acc_ref[...].T
        gate2 = cond_ref[0][:, 5 * D_MODEL:6 * D_MODEL]
        out_ref[...] = _layer_norm(alpha * x1_ref[...] + (1.0 + gate2) * ffn,
                                   lng_ref[...], lnb_ref[...])


def _peer_call(u2t, rank2, e2, l1, e1, u_bf, vt_bf, x1_flat, cond3, ln_g, ln_b, seqlen, alpha):
    d, t = u2t.shape
    n_exp = u_bf.shape[0]
    tm, te = PEER_TM, PEER_TE
    per_b = seqlen // tm
    sub = PEER_SUB_I * PEER_NKEYS
    n_sub = te // sub
    assert vt_bf.shape == (n_exp // sub, d, sub)
    kern = functools.partial(_peer_kernel, alpha, n_sub, PEER_SUB_I)
    score_spec = pl.BlockSpec((PEER_HEADS, PEER_NKEYS, tm), lambda i, e: (0, 0, i))
    stage_f32 = pltpu.VMEM((sub, tm), F32)
    stage_bf16 = pltpu.VMEM((sub, tm), BF16)
    return pl.pallas_call(
        kern,
        grid=(t // tm, n_exp // te),
        in_specs=[pl.BlockSpec((d, tm), lambda i, e: (0, i)),
                  score_spec, score_spec, score_spec, score_spec,
                  pl.BlockSpec((te, d), lambda i, e: (e, 0)),
                  pl.BlockSpec((n_sub, d, sub), lambda i, e: (e, 0, 0)),
                  pl.BlockSpec((tm, d), lambda i, e: (i, 0)),
                  pl.BlockSpec((1, 1, cond3.shape[2]), lambda i, e: (i // per_b, 0, 0)),
                  pl.BlockSpec((1, d), lambda i, e: (0, 0)),
                  pl.BlockSpec((1, d), lambda i, e: (0, 0))],
        out_specs=pl.BlockSpec((tm, d), lambda i, e: (i, 0)),
        out_shape=jax.ShapeDtypeStruct((t, d), F32),
        scratch_shapes=[pltpu.VMEM((d, tm), F32), stage_f32, stage_f32,
                        stage_bf16, stage_bf16, stage_bf16, stage_bf16],
        compiler_params=pltpu.CompilerParams(
            dimension_semantics=("arbitrary", "arbitrary"), vmem_limit_bytes=VMEM_LIMIT),
        name="peer",
    )(u2t, rank2, e2, l1, e1, u_bf, vt_bf, x1_flat, cond3, ln_g, ln_b)


def _mixer_constants():
    ch = jnp.arange(D_MODEL)
    hd = jnp.arange(LANES)
    g16 = (ch[:, None] // HEAD_DIM == hd[None, :]).astype(BF16)
    e16 = (hd[:, None] == ch[None, :] // HEAD_DIM).astype(BF16)
    wide = jnp.arange(SSM_HEADS * LANES)
    e128 = (hd[:, None] == wide[None, :] // LANES).astype(BF16)
    tt = jnp.arange(SSM_CHUNK)
    tri = (tt[:, None] >= tt[None, :]).astype(BF16)
    return g16, e16, e128, tri


def _row(v):
    return v.reshape(1, -1).astype(F32)


def _pad_lanes(v):
    v = v.reshape(1, -1).astype(F32)
    return jnp.pad(v, ((0, 0), (0, LANES - v.shape[1])))


def _layer(x, cond3, alpha, w_in, sc_conv_w, ssm_conv_w, ssm_conv_b, dt_bias, a_log, d_skip,
           sc_norm_w, ssm_norm_w, w_out, ln1_g, ln1_b, w_query, sub_keys1, sub_keys2,
           expert_u, expert_v, ln2_g, ln2_b):
    bsz, seqlen, d = x.shape
    o_z = 3 * D_MODEL
    o_xbc = o_z + D_MODEL
    o_dt = o_xbc + SSM_CONV_DIM
    w_dt = jnp.pad(w_in[:, o_dt:], ((0, 0), (0, LANES - SSM_HEADS)))
    weights = [w_in[:, :o_z].astype(BF16), w_in[:, o_z:o_xbc].astype(BF16),
               w_in[:, o_xbc:o_dt].astype(BF16), w_dt.astype(BF16),
               sc_conv_w.astype(F32), ssm_conv_w.astype(F32), _row(ssm_conv_b),
               _pad_lanes(dt_bias), _pad_lanes(a_log), _row(jnp.repeat(d_skip, HEAD_DIM)),
               _row(sc_norm_w), _row(ssm_norm_w), w_out.astype(BF16), _row(ln1_g), _row(ln1_b),
               *_mixer_constants()]
    x1 = _mixer_call(x, cond3, weights, alpha)
    x1_flat = x1.reshape(bsz * seqlen, d)
    u2t, rank2, e2, l1, e1 = _prep_call(x1_flat, cond3, w_query.T.astype(BF16),
                                        sub_keys1.astype(BF16), sub_keys2.astype(BF16), seqlen)
    sub = PEER_SUB_I * PEER_NKEYS
    vt = expert_v.astype(BF16).reshape(-1, sub, d).transpose(0, 2, 1)
    out = _peer_call(u2t, rank2, e2, l1, e1, expert_u.astype(BF16), vt,
                     x1_flat, cond3, _row(ln2_g), _row(ln2_b), seqlen, alpha)
    return out.reshape(bsz, seqlen, d)


def kernel(x, c, w_cond, b_cond, w_in, sc_conv_w, ssm_conv_w, ssm_conv_b, dt_bias, a_log, d_skip,
           sc_norm_w, ssm_norm_w, w_out, ln1_g, ln1_b, w_query, sub_keys1, sub_keys2, expert_u,
           expert_v, ln2_g, ln2_b):
    depth = w_cond.shape[0]
    bsz = x.shape[0]
    alpha = (2.0 * depth) ** 0.25
    c_pad = jnp.pad(c, ((0, (-bsz) % CARRY), (0, 0)))
    for l in range(depth):
        cond = _cond_call(c_pad, w_cond[l], b_cond[l].reshape(1, -1))[:bsz]
        cond3 = cond.reshape(bsz, 1, -1)
        x = _layer(x, cond3, alpha, w_in[l], sc_conv_w[l], ssm_conv_w[l], ssm_conv_b[l], dt_bias[l],
                   a_log[l], d_skip[l], sc_norm_w[l], ssm_norm_w[l], w_out[l], ln1_g[l], ln1_b[l],
                   w_query[l], sub_keys1[l], sub_keys2[l], expert_u[l], expert_v[l], ln2_g[l],
                   ln2_b[l])
    return x
```

```python
import functools

import jax
import jax.numpy as jnp
from jax import lax
from jax.experimental import pallas as pl
from jax.experimental.pallas import tpu as pltpu

F32 = jnp.float32
BF16 = jnp.bfloat16

D_MODEL = 1024
HEAD_DIM = 64
SC_HEADS = 16
SC_WIN = 3
SSM_HEADS = 16
SSM_GROUPS = 2
SSM_STATE = 128
SSM_CONV = 4
SSM_CHUNK = 128
SSM_GN = SSM_GROUPS * SSM_STATE
SSM_CONV_DIM = D_MODEL + 2 * SSM_GN
GROUP_W = D_MODEL // SSM_GROUPS
PEER_HEADS = 8
PEER_NKEYS = 128
PEER_TOPK = 16
PEER_HALF = 128
PEER_DKEY = 256
LN_EPS = 1e-5
RMS_EPS = 1e-6
LANES = 128
SUBLANES = 8
CARRY = SUBLANES
NEG_INF = float("-inf")

MIX_TS = 256
PREP_TM = 256
PEER_TM = 512
PEER_TE = 2048
PEER_SUB_I = 2
VMEM_LIMIT = 56 * 1024 * 1024


def _dot(a, b):
    return jnp.dot(a, b, preferred_element_type=F32)


def _dot_nt(a, b):
    return lax.dot_general(a, b, (((1,), (1,)), ((), ())), preferred_element_type=F32)


def _dot_tn(a, b):
    return lax.dot_general(a, b, (((0,), (0,)), ((), ())), preferred_element_type=F32)


def _split_dot_lhs(a, m, terms):
    out = None
    r = a
    for _ in range(terms):
        hi = r.astype(BF16)
        part = _dot(hi, m)
        out = part if out is None else out + part
        r = r - hi.astype(F32)
    return out


def _split_dot_rhs(m, a, terms):
    out = None
    r = a
    for _ in range(terms):
        hi = r.astype(BF16)
        part = _dot(m, hi)
        out = part if out is None else out + part
        r = r - hi.astype(F32)
    return out


def _sigmoid(x):
    return 1.0 / (1.0 + jnp.exp(-x))


def _silu(x):
    return x * _sigmoid(x)


def _softplus(x):
    return jnp.maximum(x, 0.0) + jnp.log(1.0 + jnp.exp(-jnp.abs(x)))


def _layer_norm(v, g, b):
    mu = jnp.mean(v, axis=-1, keepdims=True)
    d = v - mu
    var = jnp.mean(d * d, axis=-1, keepdims=True)
    return d * lax.rsqrt(var + LN_EPS) * g + b


def _cond_kernel(c_ref, w_ref, b_ref, o_ref):
    c = c_ref[...]
    o_ref[...] = jnp.dot(_silu(c), w_ref[...], preferred_element_type=F32,
                         precision=lax.Precision.HIGHEST) + b_ref[...]


def _cond_call(c_pad, w_cond, b_cond):
    rows, d = c_pad.shape
    n = w_cond.shape[1]
    bn = 1024
    return pl.pallas_call(
        _cond_kernel,
        grid=(n // bn,),
        in_specs=[pl.BlockSpec((rows, d), lambda j: (0, 0)),
                  pl.BlockSpec((d, bn), lambda j: (0, j)),
                  pl.BlockSpec((1, bn), lambda j: (0, j))],
        out_specs=pl.BlockSpec((rows, bn), lambda j: (0, j)),
        out_shape=jax.ShapeDtypeStruct((rows, n), F32),
        name="cond",
    )(c_pad, w_cond, b_cond)


def _mixer_kernel(alpha, ts,
                  x_ref, cond_ref, wsc_ref, wz_ref, wxbc_ref, wdt_ref, scw_ref, xcw_ref, xcb_ref,
                  dtb_ref, alog_ref, dskip_ref, scnw_ref, ssnw_ref, wout_ref, lng_ref, lnb_ref,
                  g16_ref, e16_ref, e128_ref, tri_ref,
                  out_ref,
                  pbuf, xbuf, state, xdt_s, bc_s, adt_s, ybuf):
    s_idx = pl.program_id(1)

    @pl.when(s_idx == 0)
    def _():
        pbuf[0:CARRY, :] = jnp.zeros((CARRY, D_MODEL), F32)
        xbuf[0:CARRY, :] = jnp.zeros((CARRY, SSM_CONV_DIM), F32)
        state[...] = jnp.zeros(state.shape, F32)

    x = x_ref[0]
    cond = cond_ref[0]
    shift1 = cond[:, 0:D_MODEL]
    scale1 = cond[:, D_MODEL:2 * D_MODEL]
    gate1 = cond[:, 2 * D_MODEL:3 * D_MODEL]
    u1 = (x * (1.0 + scale1) + shift1).astype(BF16)

    sc = _dot(u1, wsc_ref[...])
    p = sc[:, D_MODEL:2 * D_MODEL] * sc[:, 2 * D_MODEL:3 * D_MODEL]
    pbuf[CARRY:CARRY + ts, :] = p
    conv = p * scw_ref[SC_WIN - 1:SC_WIN, :]
    for k in range(SC_WIN - 1):
        off = CARRY - (SC_WIN - 1) + k
        conv = conv + pbuf[off:off + ts, :] * scw_ref[k:k + 1, :]
    pbuf[0:CARRY, :] = pbuf[ts:ts + CARRY, :]
    ysc = sc[:, 0:D_MODEL] * conv
    ss = _dot((ysc * ysc).astype(BF16), g16_ref[...])
    r = lax.rsqrt(ss * (1.0 / HEAD_DIM) + RMS_EPS)
    ysc = ysc * _split_dot_lhs(r, e16_ref[...], 2) * scnw_ref[...]

    z = _dot(u1, wz_ref[...])
    xbuf[CARRY:CARRY + ts, :] = _dot(u1, wxbc_ref[...])
    xc = xcb_ref[...] + xbuf[CARRY:CARRY + ts, :] * xcw_ref[SSM_CONV - 1:SSM_CONV, :]
    for k in range(SSM_CONV - 1):
        off = CARRY - (SSM_CONV - 1) + k
        xc = xc + xbuf[off:off + ts, :] * xcw_ref[k:k + 1, :]
    xbuf[0:CARRY, :] = xbuf[ts:ts + CARRY, :]
    xc = _silu(xc)
    xs = xc[:, 0:D_MODEL]
    bc_s[...] = xc[:, D_MODEL:SSM_CONV_DIM]

    lane = lax.broadcasted_iota(jnp.int32, (1, LANES), 1)
    head_lane = lane < SSM_HEADS
    dt = jnp.where(head_lane, _softplus(_dot(u1, wdt_ref[...]) + dtb_ref[...]), 0.0)
    a_neg = jnp.where(head_lane, -jnp.exp(alog_ref[...]), 0.0)
    adt_s[...] = dt * a_neg
    xdt_s[...] = xs * _split_dot_lhs(dt, e16_ref[...], 2)

    row_i = lax.broadcasted_iota(jnp.int32, (SSM_CHUNK, SSM_CHUNK), 0)
    col_i = lax.broadcasted_iota(jnp.int32, (SSM_CHUNK, SSM_CHUNK), 1)
    causal = row_i >= col_i
    lane_p = lax.broadcasted_iota(jnp.int32, (SSM_CHUNK, LANES), 1)
    first_head = lane_p < HEAD_DIM

    def chunk_body(c, carry):
        r0 = pl.multiple_of(c * SSM_CHUNK, SSM_CHUNK)
        rows = pl.ds(r0, SSM_CHUNK)
        adt = adt_s[rows, :]
        acum = _split_dot_rhs(tri_ref[...], adt, 3)
        acum_t = acum.T
        acum_full = _split_dot_lhs(acum, e16_ref[...], 2)
        acum_b = _split_dot_lhs(acum, e128_ref[...], 2)
        alast_full = acum_full[SSM_CHUNK - 1:SSM_CHUNK, :]
        dfs_full = jnp.exp(acum_full)
        dte_full = jnp.exp(alast_full - acum_full)
        cdec_full = jnp.exp(alast_full)
        xdt = xdt_s[rows, :]
        for g in range(SSM_GROUPS):
            gl = slice(g * GROUP_W, (g + 1) * GROUP_W)
            b_g = bc_s[rows, g * SSM_STATE:(g + 1) * SSM_STATE].astype(BF16)
            c_g = bc_s[rows, SSM_GN + g * SSM_STATE:SSM_GN + (g + 1) * SSM_STATE].astype(BF16)
            cb = _dot_nt(c_g, b_g)
            st = state[g]
            y_off = _dot(c_g, st.astype(BF16)) * dfs_full[:, gl]
            xw = (xdt[:, gl] * dte_full[:, gl]).astype(BF16)
            state[g] = st * cdec_full[:, gl] + _dot_tn(b_g, xw)
            for pr in range(GROUP_W // LANES):
                h0 = g * (SSM_HEADS // SSM_GROUPS) + 2 * pr
                ms = []
                for h in (h0, h0 + 1):
                    seg = acum_b[:, h * LANES:(h + 1) * LANES] - acum_t[h:h + 1, :]
                    dec = jnp.where(causal, jnp.exp(jnp.minimum(seg, 0.0)), 0.0)
                    ms.append((cb * dec).astype(BF16))
                lhs = jnp.concatenate(ms, axis=1)
                pl_ = slice(g * GROUP_W + pr * LANES, g * GROUP_W + (pr + 1) * LANES)
                xp = xdt[:, pl_]
                rhs = jnp.concatenate([jnp.where(first_head, xp, 0.0),
                                       jnp.where(first_head, 0.0, xp)], axis=0).astype(BF16)
                y_pair = _dot(lhs, rhs) + y_off[:, pr * LANES:(pr + 1) * LANES]
                ybuf[rows, pl_] = y_pair
        return carry

    lax.fori_loop(0, ts // SSM_CHUNK, chunk_body, 0)

    y = ybuf[...] + xs * dskip_ref[...]
    y = y * _silu(z)
    parts = []
    for g in range(SSM_GROUPS):
        yg = y[:, g * GROUP_W:(g + 1) * GROUP_W]
        ms = jnp.mean(yg * yg, axis=-1, keepdims=True)
        parts.append(yg * lax.rsqrt(ms + RMS_EPS))
    y = jnp.concatenate(parts, axis=1) * ssnw_ref[...]

    ycat = jnp.concatenate([ysc.astype(BF16), y.astype(BF16)], axis=1)
    mix = _dot(ycat, wout_ref[...])
    out_ref[0] = _layer_norm(alpha * x + (1.0 + gate1) * mix, lng_ref[...], lnb_ref[...])


def _const_spec(shape):
    nd = len(shape)
    return pl.BlockSpec(shape, lambda b, s: (0,) * nd, pipeline_mode=pl.Buffered(1))


def _mixer_call(x, cond3, weights, alpha):
    bsz, seqlen, d = x.shape
    ts = MIX_TS
    kern = functools.partial(_mixer_kernel, alpha, ts)
    in_specs = [pl.BlockSpec((1, ts, d), lambda b, s: (b, s, 0)),
                pl.BlockSpec((1, 1, cond3.shape[2]), lambda b, s: (b, 0, 0))]
    in_specs += [_const_spec(w.shape) for w in weights]
    return pl.pallas_call(
        kern,
        grid=(bsz, seqlen // ts),
        in_specs=in_specs,
        out_specs=pl.BlockSpec((1, ts, d), lambda b, s: (b, s, 0)),
        out_shape=jax.ShapeDtypeStruct((bsz, seqlen, d), F32),
        scratch_shapes=[pltpu.VMEM((ts + CARRY, D_MODEL), F32),
                        pltpu.VMEM((ts + CARRY, SSM_CONV_DIM), F32),
                        pltpu.VMEM((SSM_GROUPS, SSM_STATE, GROUP_W), F32),
                        pltpu.VMEM((ts, D_MODEL), F32),
                        pltpu.VMEM((ts, 2 * SSM_GN), F32),
                        pltpu.VMEM((ts, LANES), F32),
                        pltpu.VMEM((ts, D_MODEL), F32)],
        compiler_params=pltpu.CompilerParams(
            dimension_semantics=("arbitrary", "arbitrary"), vmem_limit_bytes=VMEM_LIMIT),
        name="mixer",
    )(x, cond3, *weights)


def _sort_network(n):
    pairs = []

    def merge(lo, m, r):
        step = 2 * r
        if step < m:
            merge(lo, m, step)
            merge(lo + r, m, step)
            pairs.extend((i, i + r) for i in range(lo + r, lo + m - r, step))
        else:
            pairs.append((lo, lo + r))

    def sort(lo, m):
        if m > 1:
            sort(lo, m // 2)
            sort(lo + m // 2, m // 2)
            merge(lo, m, 1)

    sort(0, n)
    return pairs


_SORT16 = _sort_network(PEER_TOPK)


def _top_values(slabs):
    v = list(slabs)
    for i, j in _SORT16:
        v[i], v[j] = jnp.maximum(v[i], v[j]), jnp.minimum(v[i], v[j])
    vals = []
    for k in range(PEER_TOPK):
        mx = jnp.max(v[0], axis=0, keepdims=True)
        vals.append(mx)
        if k + 1 < PEER_TOPK:
            hit = v[0] == mx
            for r in range(PEER_TOPK - 1 - k):
                v[r] = jnp.where(hit, v[r + 1], v[r])
    return vals


def _slabs(x):
    return [x[r * SUBLANES:(r + 1) * SUBLANES] for r in range(x.shape[0] // SUBLANES)]


def _stack_rows(rows, row_idx):
    out = rows[-1]
    for r in range(len(rows) - 2, -1, -1):
        out = jnp.where(row_idx == r, rows[r], out)
    return out


def _prep_kernel(x1_ref, cond_ref, wqt_ref, k1_ref, k2_ref, u2t_ref, rank2_ref, e2_ref, l1_ref, e1_ref):
    x1 = x1_ref[...]
    cond = cond_ref[0]
    shift2 = cond[:, 3 * D_MODEL:4 * D_MODEL]
    scale2 = cond[:, 4 * D_MODEL:5 * D_MODEL]
    u2t = (x1 * (1.0 + scale2) + shift2).T.astype(BF16)
    u2t_ref[...] = u2t
    q_t = _dot(wqt_ref[...], u2t)
    row16 = lax.broadcasted_iota(jnp.int32, (PEER_TOPK, 1), 0)
    for h in range(PEER_HEADS):
        q1 = q_t[h * PEER_DKEY:h * PEER_DKEY + PEER_HALF].astype(BF16)
        q2 = q_t[h * PEER_DKEY + PEER_HALF:(h + 1) * PEER_DKEY].astype(BF16)
        s1 = _dot(k1_ref[h], q1)
        s2 = _dot(k2_ref[h], q2)
        a = _top_values(_slabs(s1))
        b = _top_values(_slabs(s2))
        a_st = _stack_rows(a, row16)
        b_st = _stack_rows(b, row16)
        pieces = [a[r] + b_st for r in range(4)]
        pieces += [jnp.where(row16 >= 4, b[c] + a_st, NEG_INF) for c in range(3)]
        cand = [sl for pc in pieces for sl in _slabs(pc)]
        cand += [jnp.full(cand[0].shape, NEG_INF, F32)] * (PEER_TOPK - len(cand))
        tau = _top_values(cand)[PEER_TOPK - 1]
        rank2 = jnp.full(s2.shape, float(PEER_TOPK), F32)
        for r in range(PEER_TOPK):
            rank2 = jnp.where(s2 == b[r], float(r), rank2)
        m = a[0] + b[0]
        zsum = None
        for pc in pieces:
            e = jnp.sum(jnp.where(pc >= tau, jnp.exp(pc - m), 0.0), axis=0, keepdims=True)
            zsum = e if zsum is None else zsum + e
        lmap = jnp.zeros(s1.shape, F32)
        for r in range(PEER_TOPK):
            cnt = jnp.sum(jnp.where(a[r] + b_st >= tau, 1.0, 0.0), axis=0, keepdims=True)
            lmap = jnp.where(s1 == a[r], cnt, lmap)
        rank2_ref[h] = rank2.astype(BF16)
        e2_ref[h] = jnp.exp(s2 - b[0]).astype(BF16)
        l1_ref[h] = lmap
        e1_ref[h] = jnp.exp(s1 - a[0]) * (0.5 / zsum)


def _prep_call(x1_flat, cond3, wqt, k1, k2, seqlen):
    t, d = x1_flat.shape
    tm = PREP_TM
    per_b = seqlen // tm
    score_spec = pl.BlockSpec((PEER_HEADS, PEER_NKEYS, tm), lambda i: (0, 0, i))
    return pl.pallas_call(
        _prep_kernel,
        grid=(t // tm,),
        in_specs=[pl.BlockSpec((tm, d), lambda i: (i, 0)),
                  pl.BlockSpec((1, 1, cond3.shape[2]), lambda i: (i // per_b, 0, 0)),
                  pl.BlockSpec(wqt.shape, lambda i: (0, 0), pipeline_mode=pl.Buffered(1)),
                  pl.BlockSpec(k1.shape, lambda i: (0, 0, 0), pipeline_mode=pl.Buffered(1)),
                  pl.BlockSpec(k2.shape, lambda i: (0, 0, 0), pipeline_mode=pl.Buffered(1))],
        out_specs=[pl.BlockSpec((d, tm), lambda i: (0, i)),
                   score_spec, score_spec, score_spec, score_spec],
        out_shape=[jax.ShapeDtypeStruct((d, t), BF16),
                   jax.ShapeDtypeStruct((PEER_HEADS, PEER_NKEYS, t), BF16),
                   jax.ShapeDtypeStruct((PEER_HEADS, PEER_NKEYS, t), BF16),
                   jax.ShapeDtypeStruct((PEER_HEADS, PEER_NKEYS, t), F32),
                   jax.ShapeDtypeStruct((PEER_HEADS, PEER_NKEYS, t), F32)],
        compiler_params=pltpu.CompilerParams(
            dimension_semantics=("arbitrary",), vmem_limit_bytes=VMEM_LIMIT),
        name="peer_prep",
    )(x1_flat, cond3, wqt, k1, k2)


def _peer_kernel(alpha, n_sub, sub_i,
                 u2t_ref, rank2_ref, e2_ref, l1_ref, e1_ref, u_ref, vt_ref, x1_ref,
                 cond_ref, lng_ref, lnb_ref, out_ref,
                 acc_ref, h_even, h_odd, w_even, w_odd, act_even, act_odd):
    e_idx = pl.program_id(1)

    @pl.when(e_idx == 0)
    def _():
        acc_ref[...] = jnp.zeros(acc_ref.shape, F32)

    sub = sub_i * PEER_NKEYS
    h_bufs, w_bufs, act_bufs = (h_even, h_odd), (w_even, w_odd), (act_even, act_odd)

    def first_matmul(k):
        h_bufs[k % 2][...] = _dot(u_ref[k * sub:(k + 1) * sub, :], u2t_ref[...])

    def gates(k):
        for ii in range(sub_i):
            i = e_idx * (n_sub * sub_i) + k * sub_i + ii
            w = None
            for h in range(PEER_HEADS):
                lrow = l1_ref[h, pl.ds(i, 1), :].astype(BF16)
                erow = e1_ref[h, pl.ds(i, 1), :].astype(BF16)
                wh = jnp.where(rank2_ref[h] < lrow, e2_ref[h], 0.0) * erow
                w = wh if w is None else w + wh
            w_bufs[k % 2][ii * PEER_NKEYS:(ii + 1) * PEER_NKEYS, :] = w

    def activate(k):
        hh = h_bufs[k % 2][...]
        gelu2 = hh * (1.0 + lax.erf(hh * 0.7071067811865476))
        act_bufs[k % 2][...] = gelu2.astype(BF16) * w_bufs[k % 2][...]

    def second_matmul(k):
        acc_ref[...] += _dot(vt_ref[k], act_bufs[k % 2][...])

    first_matmul(0)
    gates(0)
    for k in range(n_sub):
        if k + 1 < n_sub:
            first_matmul(k + 1)
            gates(k + 1)
        activate(k)
        if k >= 1:
            second_matmul(k - 1)
    second_matmul(n_sub - 1)

    @pl.when(e_idx == pl.num_programs(1) - 1)
    def _():
        ffn = acc_ref[...].T
        gate2 = cond_ref[0][:, 5 * D_MODEL:6 * D_MODEL]
        out_ref[...] = _layer_norm(alpha * x1_ref[...] + (1.0 + gate2) * ffn,
                                   lng_ref[...], lnb_ref[...])


def _peer_call(u2t, rank2, e2, l1, e1, u_bf, vt_bf, x1_flat, cond3, ln_g, ln_b, seqlen, alpha):
    d, t = u2t.shape
    n_exp = u_bf.shape[0]
    tm, te = PEER_TM, PEER_TE
    per_b = seqlen // tm
    sub = PEER_SUB_I * PEER_NKEYS
    n_sub = te // sub
    assert vt_bf.shape == (n_exp // sub, d, sub)
    kern = functools.partial(_peer_kernel, alpha, n_sub, PEER_SUB_I)
    score_spec = pl.BlockSpec((PEER_HEADS, PEER_NKEYS, tm), lambda i, e: (0, 0, i))
    stage_f32 = pltpu.VMEM((sub, tm), F32)
    stage_bf16 = pltpu.VMEM((sub, tm), BF16)
    return pl.pallas_call(
        kern,
        grid=(t // tm, n_exp // te),
        in_specs=[pl.BlockSpec((d, tm), lambda i, e: (0, i)),
                  score_spec, score_spec, score_spec, score_spec,
                  pl.BlockSpec((te, d), lambda i, e: (e, 0)),
                  pl.BlockSpec((n_sub, d, sub), lambda i, e: (e, 0, 0)),
                  pl.BlockSpec((tm, d), lambda i, e: (i, 0)),
                  pl.BlockSpec((1, 1, cond3.shape[2]), lambda i, e: (i // per_b, 0, 0)),
                  pl.BlockSpec((1, d), lambda i, e: (0, 0)),
                  pl.BlockSpec((1, d), lambda i, e: (0, 0))],
        out_specs=pl.BlockSpec((tm, d), lambda i, e: (i, 0)),
        out_shape=jax.ShapeDtypeStruct((t, d), F32),
        scratch_shapes=[pltpu.VMEM((d, tm), F32), stage_f32, stage_f32,
                        stage_bf16, stage_bf16, stage_bf16, stage_bf16],
        compiler_params=pltpu.CompilerParams(
            dimension_semantics=("arbitrary", "arbitrary"), vmem_limit_bytes=VMEM_LIMIT),
        name="peer",
    )(u2t, rank2, e2, l1, e1, u_bf, vt_bf, x1_flat, cond3, ln_g, ln_b)


def _mixer_constants():
    ch = jnp.arange(D_MODEL)
    hd = jnp.arange(LANES)
    g16 = (ch[:, None] // HEAD_DIM == hd[None, :]).astype(BF16)
    e16 = (hd[:, None] == ch[None, :] // HEAD_DIM).astype(BF16)
    wide = jnp.arange(SSM_HEADS * LANES)
    e128 = (hd[:, None] == wide[None, :] // LANES).astype(BF16)
    tt = jnp.arange(SSM_CHUNK)
    tri = (tt[:, None] >= tt[None, :]).astype(BF16)
    return g16, e16, e128, tri


def _row(v):
    return v.reshape(1, -1).astype(F32)


def _pad_lanes(v):
    v = v.reshape(1, -1).astype(F32)
    return jnp.pad(v, ((0, 0), (0, LANES - v.shape[1])))


def _layer(x, cond3, alpha, w_in, sc_conv_w, ssm_conv_w, ssm_conv_b, dt_bias, a_log, d_skip,
           sc_norm_w, ssm_norm_w, w_out, ln1_g, ln1_b, w_query, sub_keys1, sub_keys2,
           expert_u, expert_v, ln2_g, ln2_b):
    bsz, seqlen, d = x.shape
    o_z = 3 * D_MODEL
    o_xbc = o_z + D_MODEL
    o_dt = o_xbc + SSM_CONV_DIM
    w_dt = jnp.pad(w_in[:, o_dt:], ((0, 0), (0, LANES - SSM_HEADS)))
    weights = [w_in[:, :o_z].astype(BF16), w_in[:, o_z:o_xbc].astype(BF16),
               w_in[:, o_xbc:o_dt].astype(BF16), w_dt.astype(BF16),
               sc_conv_w.astype(F32), ssm_conv_w.astype(F32), _row(ssm_conv_b),
               _pad_lanes(dt_bias), _pad_lanes(a_log), _row(jnp.repeat(d_skip, HEAD_DIM)),
               _row(sc_norm_w), _row(ssm_norm_w), w_out.astype(BF16), _row(ln1_g), _row(ln1_b),
               *_mixer_constants()]
    x1 = _mixer_call(x, cond3, weights, alpha)
    x1_flat = x1.reshape(bsz * seqlen, d)
    u2t, rank2, e2, l1, e1 = _prep_call(x1_flat, cond3, w_query.T.astype(BF16),
                                        sub_keys1.astype(BF16), sub_keys2.astype(BF16), seqlen)
    sub = PEER_SUB_I * PEER_NKEYS
    vt = expert_v.astype(BF16).reshape(-1, sub, d).transpose(0, 2, 1)
    out = _peer_call(u2t, rank2, e2, l1, e1, expert_u.astype(BF16), vt,
                     x1_flat, cond3, _row(ln2_g), _row(ln2_b), seqlen, alpha)
    return out.reshape(bsz, seqlen, d)


def kernel(x, c, w_cond, b_cond, w_in, sc_conv_w, ssm_conv_w, ssm_conv_b, dt_bias, a_log, d_skip,
           sc_norm_w, ssm_norm_w, w_out, ln1_g, ln1_b, w_query, sub_keys1, sub_keys2, expert_u,
           expert_v, ln2_g, ln2_b):
    depth = w_cond.shape[0]
    bsz = x.shape[0]
    alpha = (2.0 * depth) ** 0.25
    c_pad = jnp.pad(c, ((0, (-bsz) % CARRY), (0, 0)))
    for l in range(depth):
        cond = _cond_call(c_pad, w_cond[l], b_cond[l].reshape(1, -1))[:bsz]
        cond3 = cond.reshape(bsz, 1, -1)
        x = _layer(x, cond3, alpha, w_in[l], sc_conv_w[l], ssm_conv_w[l], ssm_conv_b[l], dt_bias[l],
                   a_log[l], d_skip[l], sc_norm_w[l], ssm_norm_w[l], w_out[l], ln1_g[l], ln1_b[l],
                   w_query[l], sub_keys1[l], sub_keys2[l], expert_u[l], expert_v[l], ln2_g[l],
                   ln2_b[l])
    return x
```

```python
import functools

import jax
import jax.numpy as jnp
from jax import lax
from jax.experimental import pallas as pl
from jax.experimental.pallas import tpu as pltpu

F32 = jnp.float32
BF16 = jnp.bfloat16

D_MODEL = 1024
HEAD_DIM = 64
SC_HEADS = 16
SC_WIN = 3
SSM_HEADS = 16
SSM_GROUPS = 2
SSM_STATE = 128
SSM_CONV = 4
SSM_CHUNK = 128
SSM_GN = SSM_GROUPS * SSM_STATE
SSM_CONV_DIM = D_MODEL + 2 * SSM_GN
GROUP_W = D_MODEL // SSM_GROUPS
PEER_HEADS = 8
PEER_NKEYS = 128
PEER_TOPK = 16
PEER_HALF = 128
PEER_DKEY = 256
LN_EPS = 1e-5
RMS_EPS = 1e-6
LANES = 128
SUBLANES = 8
CARRY = SUBLANES
NEG_INF = float("-inf")

MIX_TS = 512
PREP_TM = 256
PEER_TM = 512
PEER_TE = 2048
PEER_SUB_I = 4
VMEM_LIMIT = 56 * 1024 * 1024


def _dot(a, b):
    return jnp.dot(a, b, preferred_element_type=F32)


def _dot_nt(a, b):
    return lax.dot_general(a, b, (((1,), (1,)), ((), ())), preferred_element_type=F32)


def _dot_tn(a, b):
    return lax.dot_general(a, b, (((0,), (0,)), ((), ())), preferred_element_type=F32)


def _split_dot_lhs(a, m, terms):
    out = None
    r = a
    for _ in range(terms):
        hi = r.astype(BF16)
        part = _dot(hi, m)
        out = part if out is None else out + part
        r = r - hi.astype(F32)
    return out


def _split_dot_rhs(m, a, terms):
    out = None
    r = a
    for _ in range(terms):
        hi = r.astype(BF16)
        part = _dot(m, hi)
        out = part if out is None else out + part
        r = r - hi.astype(F32)
    return out


def _sigmoid(x):
    return 1.0 / (1.0 + jnp.exp(-x))


def _silu(x):
    return x * _sigmoid(x)


def _softplus(x):
    return jnp.maximum(x, 0.0) + jnp.log(1.0 + jnp.exp(-jnp.abs(x)))


def _layer_norm(v, g, b):
    mu = jnp.mean(v, axis=-1, keepdims=True)
    d = v - mu
    var = jnp.mean(d * d, axis=-1, keepdims=True)
    return d * lax.rsqrt(var + LN_EPS) * g + b


def _cond_kernel(c_ref, w_ref, b_ref, o_ref):
    c = c_ref[...]
    o_ref[...] = jnp.dot(_silu(c), w_ref[...], preferred_element_type=F32,
                         precision=lax.Precision.HIGHEST) + b_ref[...]


def _cond_call(c_pad, w_cond, b_cond):
    rows, d = c_pad.shape
    n = w_cond.shape[1]
    bn = 1024
    return pl.pallas_call(
        _cond_kernel,
        grid=(n // bn,),
        in_specs=[pl.BlockSpec((rows, d), lambda j: (0, 0)),
                  pl.BlockSpec((d, bn), lambda j: (0, j)),
                  pl.BlockSpec((1, bn), lambda j: (0, j))],
        out_specs=pl.BlockSpec((rows, bn), lambda j: (0, j)),
        out_shape=jax.ShapeDtypeStruct((rows, n), F32),
        name="cond",
    )(c_pad, w_cond, b_cond)


def _mixer_kernel(alpha, ts,
                  x_ref, cond_ref, wsc_ref, wz_ref, wxbc_ref, wdt_ref, scw_ref, xcw_ref, xcb_ref,
                  dtb_ref, alog_ref, dskip_ref, scnw_ref, ssnw_ref, wout_ref, lng_ref, lnb_ref,
                  g16_ref, e16_ref, e128_ref, tri_ref,
                  out_ref,
                  pbuf, xbuf, state, xdt_s, bc_s, adt_s, ybuf):
    s_idx = pl.program_id(1)

    @pl.when(s_idx == 0)
    def _():
        pbuf[0:CARRY, :] = jnp.zeros((CARRY, D_MODEL), F32)
        xbuf[0:CARRY, :] = jnp.zeros((CARRY, SSM_CONV_DIM), F32)
        state[...] = jnp.zeros(state.shape, F32)

    x = x_ref[0]
    cond = cond_ref[0]
    shift1 = cond[:, 0:D_MODEL]
    scale1 = cond[:, D_MODEL:2 * D_MODEL]
    gate1 = cond[:, 2 * D_MODEL:3 * D_MODEL]
    u1 = (x * (1.0 + scale1) + shift1).astype(BF16)

    sc = _dot(u1, wsc_ref[...])
    p = sc[:, D_MODEL:2 * D_MODEL] * sc[:, 2 * D_MODEL:3 * D_MODEL]
    pbuf[CARRY:CARRY + ts, :] = p
    conv = p * scw_ref[SC_WIN - 1:SC_WIN, :]
    for k in range(SC_WIN - 1):
        off = CARRY - (SC_WIN - 1) + k
        conv = conv + pbuf[off:off + ts, :] * scw_ref[k:k + 1, :]
    pbuf[0:CARRY, :] = pbuf[ts:ts + CARRY, :]
    ysc = sc[:, 0:D_MODEL] * conv
    ss = _dot((ysc * ysc).astype(BF16), g16_ref[...])
    r = lax.rsqrt(ss * (1.0 / HEAD_DIM) + RMS_EPS)
    ysc = ysc * _split_dot_lhs(r, e16_ref[...], 2) * scnw_ref[...]

    z = _dot(u1, wz_ref[...])
    xbuf[CARRY:CARRY + ts, :] = _dot(u1, wxbc_ref[...])
    xc = xcb_ref[...] + xbuf[CARRY:CARRY + ts, :] * xcw_ref[SSM_CONV - 1:SSM_CONV, :]
    for k in range(SSM_CONV - 1):
        off = CARRY - (SSM_CONV - 1) + k
        xc = xc + xbuf[off:off + ts, :] * xcw_ref[k:k + 1, :]
    xbuf[0:CARRY, :] = xbuf[ts:ts + CARRY, :]
    xc = _silu(xc)
    xs = xc[:, 0:D_MODEL]
    bc_s[...] = xc[:, D_MODEL:SSM_CONV_DIM]

    lane = lax.broadcasted_iota(jnp.int32, (1, LANES), 1)
    head_lane = lane < SSM_HEADS
    dt = jnp.where(head_lane, _softplus(_dot(u1, wdt_ref[...]) + dtb_ref[...]), 0.0)
    a_neg = jnp.where(head_lane, -jnp.exp(alog_ref[...]), 0.0)
    adt_s[...] = dt * a_neg
    xdt_s[...] = xs * _split_dot_lhs(dt, e16_ref[...], 2)

    row_i = lax.broadcasted_iota(jnp.int32, (SSM_CHUNK, SSM_CHUNK), 0)
    col_i = lax.broadcasted_iota(jnp.int32, (SSM_CHUNK, SSM_CHUNK), 1)
    causal = row_i >= col_i
    lane_p = lax.broadcasted_iota(jnp.int32, (SSM_CHUNK, LANES), 1)
    first_head = lane_p < HEAD_DIM

    def chunk_body(c, carry):
        r0 = pl.multiple_of(c * SSM_CHUNK, SSM_CHUNK)
        rows = pl.ds(r0, SSM_CHUNK)
        adt = adt_s[rows, :]
        acum = _split_dot_rhs(tri_ref[...], adt, 3)
        acum_t = acum.T
        acum_full = _split_dot_lhs(acum, e16_ref[...], 2)
        acum_b = _split_dot_lhs(acum, e128_ref[...], 2)
        alast_full = acum_full[SSM_CHUNK - 1:SSM_CHUNK, :]
        dfs_full = jnp.exp(acum_full)
        dte_full = jnp.exp(alast_full - acum_full)
        cdec_full = jnp.exp(alast_full)
        xdt = xdt_s[rows, :]
        for g in range(SSM_GROUPS):
            gl = slice(g * GROUP_W, (g + 1) * GROUP_W)
            b_g = bc_s[rows, g * SSM_STATE:(g + 1) * SSM_STATE].astype(BF16)
            c_g = bc_s[rows, SSM_GN + g * SSM_STATE:SSM_GN + (g + 1) * SSM_STATE].astype(BF16)
            cb = _dot_nt(c_g, b_g)
            st = state[g]
            y_off = _dot(c_g, st.astype(BF16)) * dfs_full[:, gl]
            xw = (xdt[:, gl] * dte_full[:, gl]).astype(BF16)
            state[g] = st * cdec_full[:, gl] + _dot_tn(b_g, xw)
            for pr in range(GROUP_W // LANES):
                h0 = g * (SSM_HEADS // SSM_GROUPS) + 2 * pr
                ms = []
                for h in (h0, h0 + 1):
                    seg = acum_b[:, h * LANES:(h + 1) * LANES] - acum_t[h:h + 1, :]
                    dec = jnp.where(causal, jnp.exp(jnp.minimum(seg, 0.0)), 0.0)
                    ms.append((cb * dec).astype(BF16))
                lhs = jnp.concatenate(ms, axis=1)
                pl_ = slice(g * GROUP_W + pr * LANES, g * GROUP_W + (pr + 1) * LANES)
                xp = xdt[:, pl_]
                rhs = jnp.concatenate([jnp.where(first_head, xp, 0.0),
                                       jnp.where(first_head, 0.0, xp)], axis=0).astype(BF16)
                y_pair = _dot(lhs, rhs) + y_off[:, pr * LANES:(pr + 1) * LANES]
                ybuf[rows, pl_] = y_pair
        return carry

    lax.fori_loop(0, ts // SSM_CHUNK, chunk_body, 0)

    y = ybuf[...] + xs * dskip_ref[...]
    y = y * _silu(z)
    parts = []
    for g in range(SSM_GROUPS):
        yg = y[:, g * GROUP_W:(g + 1) * GROUP_W]
        ms = jnp.mean(yg * yg, axis=-1, keepdims=True)
        parts.append(yg * lax.rsqrt(ms + RMS_EPS))
    y = jnp.concatenate(parts, axis=1) * ssnw_ref[...]

    ycat = jnp.concatenate([ysc.astype(BF16), y.astype(BF16)], axis=1)
    mix = _dot(ycat, wout_ref[...])
    out_ref[0] = _layer_norm(alpha * x + (1.0 + gate1) * mix, lng_ref[...], lnb_ref[...])


def _const_spec(shape):
    nd = len(shape)
    return pl.BlockSpec(shape, lambda b, s: (0,) * nd, pipeline_mode=pl.Buffered(1))


def _mixer_call(x, cond3, weights, alpha):
    bsz, seqlen, d = x.shape
    ts = MIX_TS
    kern = functools.partial(_mixer_kernel, alpha, ts)
    in_specs = [pl.BlockSpec((1, ts, d), lambda b, s: (b, s, 0)),
                pl.BlockSpec((1, 1, cond3.shape[2]), lambda b, s: (b, 0, 0))]
    in_specs += [_const_spec(w.shape) for w in weights]
    return pl.pallas_call(
        kern,
        grid=(bsz, seqlen // ts),
        in_specs=in_specs,
        out_specs=pl.BlockSpec((1, ts, d), lambda b, s: (b, s, 0)),
        out_shape=jax.ShapeDtypeStruct((bsz, seqlen, d), F32),
        scratch_shapes=[pltpu.VMEM((ts + CARRY, D_MODEL), F32),
                        pltpu.VMEM((ts + CARRY, SSM_CONV_DIM), F32),
                        pltpu.VMEM((SSM_GROUPS, SSM_STATE, GROUP_W), F32),
                        pltpu.VMEM((ts, D_MODEL), F32),
                        pltpu.VMEM((ts, 2 * SSM_GN), F32),
                        pltpu.VMEM((ts, LANES), F32),
                        pltpu.VMEM((ts, D_MODEL), F32)],
        compiler_params=pltpu.CompilerParams(
            dimension_semantics=("arbitrary", "arbitrary"), vmem_limit_bytes=VMEM_LIMIT),
        name="mixer",
    )(x, cond3, *weights)


def _sort_network(n):
    pairs = []

    def merge(lo, m, r):
        step = 2 * r
        if step < m:
            merge(lo, m, step)
            merge(lo + r, m, step)
            pairs.extend((i, i + r) for i in range(lo + r, lo + m - r, step))
        else:
            pairs.append((lo, lo + r))

    def sort(lo, m):
        if m > 1:
            sort(lo, m // 2)
            sort(lo + m // 2, m // 2)
            merge(lo, m, 1)

    sort(0, n)
    return pairs


_SORT16 = _sort_network(PEER_TOPK)


def _top_values(slabs):
    v = list(slabs)
    for i, j in _SORT16:
        v[i], v[j] = jnp.maximum(v[i], v[j]), jnp.minimum(v[i], v[j])
    vals = []
    for k in range(PEER_TOPK):
        mx = jnp.max(v[0], axis=0, keepdims=True)
        vals.append(mx)
        if k + 1 < PEER_TOPK:
            hit = v[0] == mx
            for r in range(PEER_TOPK - 1 - k):
                v[r] = jnp.where(hit, v[r + 1], v[r])
    return vals


def _slabs(x):
    return [x[r * SUBLANES:(r + 1) * SUBLANES] for r in range(x.shape[0] // SUBLANES)]


def _stack_rows(rows, row_idx):
    out = rows[-1]
    for r in range(len(rows) - 2, -1, -1):
        out = jnp.where(row_idx == r, rows[r], out)
    return out


def _prep_kernel(x1_ref, cond_ref, wqt_ref, k1_ref, k2_ref, u2t_ref, rank2_ref, e2_ref, l1_ref, e1_ref):
    x1 = x1_ref[...]
    cond = cond_ref[0]
    shift2 = cond[:, 3 * D_MODEL:4 * D_MODEL]
    scale2 = cond[:, 4 * D_MODEL:5 * D_MODEL]
    u2t = (x1 * (1.0 + scale2) + shift2).T.astype(BF16)
    u2t_ref[...] = u2t
    q_t = _dot(wqt_ref[...], u2t)
    row16 = lax.broadcasted_iota(jnp.int32, (PEER_TOPK, 1), 0)
    for h in range(PEER_HEADS):
        q1 = q_t[h * PEER_DKEY:h * PEER_DKEY + PEER_HALF].astype(BF16)
        q2 = q_t[h * PEER_DKEY + PEER_HALF:(h + 1) * PEER_DKEY].astype(BF16)
        s1 = _dot(k1_ref[h], q1)
        s2 = _dot(k2_ref[h], q2)
        a = _top_values(_slabs(s1))
        b = _top_values(_slabs(s2))
        a_st = _stack_rows(a, row16)
        b_st = _stack_rows(b, row16)
        pieces = [a[r] + b_st for r in range(4)]
        pieces += [jnp.where(row16 >= 4, b[c] + a_st, NEG_INF) for c in range(3)]
        cand = [sl for pc in pieces for sl in _slabs(pc)]
        cand += [jnp.full(cand[0].shape, NEG_INF, F32)] * (PEER_TOPK - len(cand))
        tau = _top_values(cand)[PEER_TOPK - 1]
        rank2 = jnp.full(s2.shape, float(PEER_TOPK), F32)
        for r in range(PEER_TOPK):
            rank2 = jnp.where(s2 == b[r], float(r), rank2)
        m = a[0] + b[0]
        zsum = None
        for pc in pieces:
            e = jnp.sum(jnp.where(pc >= tau, jnp.exp(pc - m), 0.0), axis=0, keepdims=True)
            zsum = e if zsum is None else zsum + e
        lmap = jnp.zeros(s1.shape, F32)
        for r in range(PEER_TOPK):
            cnt = jnp.sum(jnp.where(a[r] + b_st >= tau, 1.0, 0.0), axis=0, keepdims=True)
            lmap = jnp.where(s1 == a[r], cnt, lmap)
        rank2_ref[h] = rank2.astype(BF16)
        e2_ref[h] = jnp.exp(s2 - b[0]).astype(BF16)
        l1_ref[h] = lmap
        e1_ref[h] = jnp.exp(s1 - a[0]) * (0.5 / zsum)


def _prep_call(x1_flat, cond3, wqt, k1, k2, seqlen):
    t, d = x1_flat.shape
    tm = PREP_TM
    per_b = seqlen // tm
    score_spec = pl.BlockSpec((PEER_HEADS, PEER_NKEYS, tm), lambda i: (0, 0, i))
    return pl.pallas_call(
        _prep_kernel,
        grid=(t // tm,),
        in_specs=[pl.BlockSpec((tm, d), lambda i: (i, 0)),
                  pl.BlockSpec((1, 1, cond3.shape[2]), lambda i: (i // per_b, 0, 0)),
                  pl.BlockSpec(wqt.shape, lambda i: (0, 0), pipeline_mode=pl.Buffered(1)),
                  pl.BlockSpec(k1.shape, lambda i: (0, 0, 0), pipeline_mode=pl.Buffered(1)),
                  pl.BlockSpec(k2.shape, lambda i: (0, 0, 0), pipeline_mode=pl.Buffered(1))],
        out_specs=[pl.BlockSpec((d, tm), lambda i: (0, i)),
                   score_spec, score_spec, score_spec, score_spec],
        out_shape=[jax.ShapeDtypeStruct((d, t), BF16),
                   jax.ShapeDtypeStruct((PEER_HEADS, PEER_NKEYS, t), BF16),
                   jax.ShapeDtypeStruct((PEER_HEADS, PEER_NKEYS, t), BF16),
                   jax.ShapeDtypeStruct((PEER_HEADS, PEER_NKEYS, t), F32),
                   jax.ShapeDtypeStruct((PEER_HEADS, PEER_NKEYS, t), F32)],
        compiler_params=pltpu.CompilerParams(
            dimension_semantics=("arbitrary",), vmem_limit_bytes=VMEM_LIMIT),
        name="peer_prep",
    )(x1_flat, cond3, wqt, k1, k2)


def _peer_kernel(alpha, n_sub, sub_i,
                 u2t_ref, rank2_ref, e2_ref, l1_ref, e1_ref, u_ref, vt_ref, x1_ref,
                 cond_ref, lng_ref, lnb_ref, out_ref,
                 acc_ref, h_even, h_odd, w_even, w_odd, act_even, act_odd):
    e_idx = pl.program_id(1)

    @pl.when(e_idx == 0)
    def _():
        acc_ref[...] = jnp.zeros(acc_ref.shape, F32)

    sub = sub_i * PEER_NKEYS
    h_bufs, w_bufs, act_bufs = (h_even, h_odd), (w_even, w_odd), (act_even, act_odd)

    def first_matmul(k):
        h_bufs[k % 2][...] = _dot(u_ref[k * sub:(k + 1) * sub, :], u2t_ref[...])

    def gates(k):
        for ii in range(sub_i):
            i = e_idx * (n_sub * sub_i) + k * sub_i + ii
            w = None
            for h in range(PEER_HEADS):
                lrow = l1_ref[h, pl.ds(i, 1), :].astype(BF16)
                erow = e1_ref[h, pl.ds(i, 1), :].astype(BF16)
                wh = jnp.where(rank2_ref[h] < lrow, e2_ref[h], 0.0) * erow
                w = wh if w is None else w + wh
            w_bufs[k % 2][ii * PEER_NKEYS:(ii + 1) * PEER_NKEYS, :] = w

    def activate(k):
        hh = h_bufs[k % 2][...]
        gelu2 = hh * (1.0 + lax.erf(hh * 0.7071067811865476))
        act_bufs[k % 2][...] = gelu2.astype(BF16) * w_bufs[k % 2][...]

    def second_matmul(k):
        acc_ref[...] += _dot(vt_ref[k], act_bufs[k % 2][...])

    first_matmul(0)
    gates(0)
    for k in range(n_sub):
        if k + 1 < n_sub:
            first_matmul(k + 1)
            gates(k + 1)
        activate(k)
        if k >= 1:
            second_matmul(k - 1)
    second_matmul(n_sub - 1)

    @pl.when(e_idx == pl.num_programs(1) - 1)
    def _():
        ffn = acc_ref[...].T
        gate2 = cond_ref[0][:, 5 * D_MODEL:6 * D_MODEL]
        out_ref[...] = _layer_norm(alpha * x1_ref[...] + (1.0 + gate2) * ffn,
                                   lng_ref[...], lnb_ref[...])


def _peer_call(u2t, rank2, e2, l1, e1, u_bf, vt_bf, x1_flat, cond3, ln_g, ln_b, seqlen, alpha):
    d, t = u2t.shape
    n_exp = u_bf.shape[0]
    tm, te = PEER_TM, PEER_TE
    per_b = seqlen // tm
    sub = PEER_SUB_I * PEER_NKEYS
    n_sub = te // sub
    assert vt_bf.shape == (n_exp // sub, d, sub)
    kern = functools.partial(_peer_kernel, alpha, n_sub, PEER_SUB_I)
    score_spec = pl.BlockSpec((PEER_HEADS, PEER_NKEYS, tm), lambda i, e: (0, 0, i))
    stage_f32 = pltpu.VMEM((sub, tm), F32)
    stage_bf16 = pltpu.VMEM((sub, tm), BF16)
    return pl.pallas_call(
        kern,
        grid=(t // tm, n_exp // te),
        in_specs=[pl.BlockSpec((d, tm), lambda i, e: (0, i)),
                  score_spec, score_spec, score_spec, score_spec,
                  pl.BlockSpec((te, d), lambda i, e: (e, 0)),
                  pl.BlockSpec((n_sub, d, sub), lambda i, e: (e, 0, 0)),
                  pl.BlockSpec((tm, d), lambda i, e: (i, 0)),
                  pl.BlockSpec((1, 1, cond3.shape[2]), lambda i, e: (i // per_b, 0, 0)),
                  pl.BlockSpec((1, d), lambda i, e: (0, 0)),
                  pl.BlockSpec((1, d), lambda i, e: (0, 0))],
        out_specs=pl.BlockSpec((tm, d), lambda i, e: (i, 0)),
        out_shape=jax.ShapeDtypeStruct((t, d), F32),
        scratch_shapes=[pltpu.VMEM((d, tm), F32), stage_f32, stage_f32,
                        stage_bf16, stage_bf16, stage_bf16, stage_bf16],
        compiler_params=pltpu.CompilerParams(
            dimension_semantics=("arbitrary", "arbitrary"), vmem_limit_bytes=VMEM_LIMIT),
        name="peer",
    )(u2t, rank2, e2, l1, e1, u_bf, vt_bf, x1_flat, cond3, ln_g, ln_b)


def _mixer_constants():
    ch = jnp.arange(D_MODEL)
    hd = jnp.arange(LANES)
    g16 = (ch[:, None] // HEAD_DIM == hd[None, :]).astype(BF16)
    e16 = (hd[:, None] == ch[None, :] // HEAD_DIM).astype(BF16)
    wide = jnp.arange(SSM_HEADS * LANES)
    e128 = (hd[:, None] == wide[None, :] // LANES).astype(BF16)
    tt = jnp.arange(SSM_CHUNK)
    tri = (tt[:, None] >= tt[None, :]).astype(BF16)
    return g16, e16, e128, tri


def _row(v):
    return v.reshape(1, -1).astype(F32)


def _pad_lanes(v):
    v = v.reshape(1, -1).astype(F32)
    return jnp.pad(v, ((0, 0), (0, LANES - v.shape[1])))


def _layer(x, cond3, alpha, w_in, sc_conv_w, ssm_conv_w, ssm_conv_b, dt_bias, a_log, d_skip,
           sc_norm_w, ssm_norm_w, w_out, ln1_g, ln1_b, w_query, sub_keys1, sub_keys2,
           expert_u, expert_v, ln2_g, ln2_b):
    bsz, seqlen, d = x.shape
    o_z = 3 * D_MODEL
    o_xbc = o_z + D_MODEL
    o_dt = o_xbc + SSM_CONV_DIM
    w_dt = jnp.pad(w_in[:, o_dt:], ((0, 0), (0, LANES - SSM_HEADS)))
    weights = [w_in[:, :o_z].astype(BF16), w_in[:, o_z:o_xbc].astype(BF16),
               w_in[:, o_xbc:o_dt].astype(BF16), w_dt.astype(BF16),
               sc_conv_w.astype(F32), ssm_conv_w.astype(F32), _row(ssm_conv_b),
               _pad_lanes(dt_bias), _pad_lanes(a_log), _row(jnp.repeat(d_skip, HEAD_DIM)),
               _row(sc_norm_w), _row(ssm_norm_w), w_out.astype(BF16), _row(ln1_g), _row(ln1_b),
               *_mixer_constants()]
    x1 = _mixer_call(x, cond3, weights, alpha)
    x1_flat = x1.reshape(bsz * seqlen, d)
    u2t, rank2, e2, l1, e1 = _prep_call(x1_flat, cond3, w_query.T.astype(BF16),
                                        sub_keys1.astype(BF16), sub_keys2.astype(BF16), seqlen)
    sub = PEER_SUB_I * PEER_NKEYS
    vt = expert_v.reshape(-1, sub, d).transpose(0, 2, 1).astype(BF16)
    out = _peer_call(u2t, rank2, e2, l1, e1, expert_u.astype(BF16), vt,
                     x1_flat, cond3, _row(ln2_g), _row(ln2_b), seqlen, alpha)
    return out.reshape(bsz, seqlen, d)


def kernel(x, c, w_cond, b_cond, w_in, sc_conv_w, ssm_conv_w, ssm_conv_b, dt_bias, a_log, d_skip,
           sc_norm_w, ssm_norm_w, w_out, ln1_g, ln1_b, w_query, sub_keys1, sub_keys2, expert_u,
           expert_v, ln2_g, ln2_b):
    depth = w_cond.shape[0]
    bsz = x.shape[0]
    alpha = (2.0 * depth) ** 0.25
    c_pad = jnp.pad(c, ((0, (-bsz) % CARRY), (0, 0)))
    for l in range(depth):
        cond = _cond_call(c_pad, w_cond[l], b_cond[l].reshape(1, -1))[:bsz]
        cond3 = cond.reshape(bsz, 1, -1)
        x = _layer(x, cond3, alpha, w_in[l], sc_conv_w[l], ssm_conv_w[l], ssm_conv_b[l], dt_bias[l],
                   a_log[l], d_skip[l], sc_norm_w[l], ssm_norm_w[l], w_out[l], ln1_g[l], ln1_b[l],
                   w_query[l], sub_keys1[l], sub_keys2[l], expert_u[l], expert_v[l], ln2_g[l],
                   ln2_b[l])
    return x
```

```python
import functools

import jax
import jax.numpy as jnp
from jax import lax
from jax.experimental import pallas as pl
from jax.experimental.pallas import tpu as pltpu

F32 = jnp.float32
BF16 = jnp.bfloat16

D_MODEL = 1024
HEAD_DIM = 64
SC_HEADS = 16
SC_WIN = 3
SSM_HEADS = 16
SSM_GROUPS = 2
SSM_STATE = 128
SSM_CONV = 4
SSM_CHUNK = 128
SSM_GN = SSM_GROUPS * SSM_STATE
SSM_CONV_DIM = D_MODEL + 2 * SSM_GN
GROUP_W = D_MODEL // SSM_GROUPS
PEER_HEADS = 8
PEER_NKEYS = 128
PEER_TOPK = 16
PEER_HALF = 128
PEER_DKEY = 256
LN_EPS = 1e-5
RMS_EPS = 1e-6
LANES = 128
SUBLANES = 8
CARRY = SUBLANES
NEG_INF = float("-inf")

MIX_TS = 512
PREP_TM = 512
PEER_TM = 512
PEER_TE = 2048
PEER_SUB_I = 4
VMEM_LIMIT = 56 * 1024 * 1024


def _dot(a, b):
    return jnp.dot(a, b, preferred_element_type=F32)


def _dot_nt(a, b):
    return lax.dot_general(a, b, (((1,), (1,)), ((), ())), preferred_element_type=F32)


def _dot_tn(a, b):
    return lax.dot_general(a, b, (((0,), (0,)), ((), ())), preferred_element_type=F32)


def _split_dot_lhs(a, m, terms):
    out = None
    r = a
    for _ in range(terms):
        hi = r.astype(BF16)
        part = _dot(hi, m)
        out = part if out is None else out + part
        r = r - hi.astype(F32)
    return out


def _split_dot_rhs(m, a, terms):
    out = None
    r = a
    for _ in range(terms):
        hi = r.astype(BF16)
        part = _dot(m, hi)
        out = part if out is None else out + part
        r = r - hi.astype(F32)
    return out


def _sigmoid(x):
    return 1.0 / (1.0 + jnp.exp(-x))


def _silu(x):
    return x * _sigmoid(x)


def _softplus(x):
    return jnp.maximum(x, 0.0) + jnp.log(1.0 + jnp.exp(-jnp.abs(x)))


def _layer_norm(v, g, b):
    mu = jnp.mean(v, axis=-1, keepdims=True)
    d = v - mu
    var = jnp.mean(d * d, axis=-1, keepdims=True)
    return d * lax.rsqrt(var + LN_EPS) * g + b


def _cond_kernel(c_ref, w_ref, b_ref, o_ref):
    c = c_ref[...]
    o_ref[...] = jnp.dot(_silu(c), w_ref[...], preferred_element_type=F32,
                         precision=lax.Precision.HIGHEST) + b_ref[...]


def _cond_call(c_pad, w_cond, b_cond):
    rows, d = c_pad.shape
    n = w_cond.shape[1]
    bn = 1024
    return pl.pallas_call(
        _cond_kernel,
        grid=(n // bn,),
        in_specs=[pl.BlockSpec((rows, d), lambda j: (0, 0)),
                  pl.BlockSpec((d, bn), lambda j: (0, j)),
                  pl.BlockSpec((1, bn), lambda j: (0, j))],
        out_specs=pl.BlockSpec((rows, bn), lambda j: (0, j)),
        out_shape=jax.ShapeDtypeStruct((rows, n), F32),
        name="cond",
    )(c_pad, w_cond, b_cond)


def _mixer_kernel(alpha, ts,
                  x_ref, cond_ref, wsc_ref, wz_ref, wxbc_ref, wdt_ref, scw_ref, xcw_ref, xcb_ref,
                  dtb_ref, alog_ref, dskip_ref, scnw_ref, ssnw_ref, wout_ref, lng_ref, lnb_ref,
                  g16_ref, e16_ref, tri_ref,
                  out_ref,
                  pbuf, xbuf, state, xdt_s, bc_s, adt_s, ybuf):
    s_idx = pl.program_id(1)

    @pl.when(s_idx == 0)
    def _():
        pbuf[0:CARRY, :] = jnp.zeros((CARRY, D_MODEL), F32)
        xbuf[0:CARRY, :] = jnp.zeros((CARRY, SSM_CONV_DIM), F32)
        state[...] = jnp.zeros(state.shape, F32)

    x = x_ref[0]
    cond = cond_ref[0]
    shift1 = cond[:, 0:D_MODEL]
    scale1 = cond[:, D_MODEL:2 * D_MODEL]
    gate1 = cond[:, 2 * D_MODEL:3 * D_MODEL]
    u1 = (x * (1.0 + scale1) + shift1).astype(BF16)

    sc = _dot(u1, wsc_ref[...])
    p = sc[:, D_MODEL:2 * D_MODEL] * sc[:, 2 * D_MODEL:3 * D_MODEL]
    pbuf[CARRY:CARRY + ts, :] = p
    conv = p * scw_ref[SC_WIN - 1:SC_WIN, :]
    for k in range(SC_WIN - 1):
        off = CARRY - (SC_WIN - 1) + k
        conv = conv + pbuf[off:off + ts, :] * scw_ref[k:k + 1, :]
    pbuf[0:CARRY, :] = pbuf[ts:ts + CARRY, :]
    ysc = sc[:, 0:D_MODEL] * conv
    ss = _dot((ysc * ysc).astype(BF16), g16_ref[...])
    r = lax.rsqrt(ss * (1.0 / HEAD_DIM) + RMS_EPS)
    ysc = ysc * _split_dot_lhs(r, e16_ref[...], 2) * scnw_ref[...]

    z = _dot(u1, wz_ref[...])
    xbuf[CARRY:CARRY + ts, :] = _dot(u1, wxbc_ref[...])
    xc = xcb_ref[...] + xbuf[CARRY:CARRY + ts, :] * xcw_ref[SSM_CONV - 1:SSM_CONV, :]
    for k in range(SSM_CONV - 1):
        off = CARRY - (SSM_CONV - 1) + k
        xc = xc + xbuf[off:off + ts, :] * xcw_ref[k:k + 1, :]
    xbuf[0:CARRY, :] = xbuf[ts:ts + CARRY, :]
    xc = _silu(xc)
    xs = xc[:, 0:D_MODEL]
    bc_s[...] = xc[:, D_MODEL:SSM_CONV_DIM]

    lane = lax.broadcasted_iota(jnp.int32, (1, LANES), 1)
    head_lane = lane < SSM_HEADS
    dt = jnp.where(head_lane, _softplus(_dot(u1, wdt_ref[...]) + dtb_ref[...]), 0.0)
    a_neg = jnp.where(head_lane, -jnp.exp(alog_ref[...]), 0.0)
    adt_s[...] = dt * a_neg
    xdt_s[...] = xs * _split_dot_lhs(dt, e16_ref[...], 2)

    row_i = lax.broadcasted_iota(jnp.int32, (SSM_CHUNK, SSM_CHUNK), 0)
    col_i = lax.broadcasted_iota(jnp.int32, (SSM_CHUNK, SSM_CHUNK), 1)
    causal = row_i >= col_i
    lane_p = lax.broadcasted_iota(jnp.int32, (SSM_CHUNK, LANES), 1)
    first_head = lane_p < HEAD_DIM

    def chunk_body(c, carry):
        r0 = pl.multiple_of(c * SSM_CHUNK, SSM_CHUNK)
        rows = pl.ds(r0, SSM_CHUNK)
        adt = adt_s[rows, :]
        acum = _split_dot_rhs(tri_ref[...], adt, 3)
        acum_t = acum.T
        acum_full = _split_dot_lhs(acum, e16_ref[...], 2)
        alast_full = acum_full[SSM_CHUNK - 1:SSM_CHUNK, :]
        dfs_full = jnp.exp(acum_full)
        dte_full = jnp.exp(alast_full - acum_full)
        cdec_full = jnp.exp(alast_full)
        xdt = xdt_s[rows, :]
        for g in range(SSM_GROUPS):
            gl = slice(g * GROUP_W, (g + 1) * GROUP_W)
            b_g = bc_s[rows, g * SSM_STATE:(g + 1) * SSM_STATE].astype(BF16)
            c_g = bc_s[rows, SSM_GN + g * SSM_STATE:SSM_GN + (g + 1) * SSM_STATE].astype(BF16)
            cb = _dot_nt(c_g, b_g)
            st = state[g]
            y_off = _dot(c_g, st.astype(BF16)) * dfs_full[:, gl]
            xw = (xdt[:, gl] * dte_full[:, gl]).astype(BF16)
            state[g] = st * cdec_full[:, gl] + _dot_tn(b_g, xw)
            for pr in range(GROUP_W // LANES):
                h0 = g * (SSM_HEADS // SSM_GROUPS) + 2 * pr
                pl_ = slice(g * GROUP_W + pr * LANES, g * GROUP_W + (pr + 1) * LANES)
                both = acum_full[:, pl_]
                swapped = pltpu.roll(both, HEAD_DIM, 1)
                per_head = (jnp.where(first_head, both, swapped), jnp.where(first_head, swapped, both))
                ms = []
                for j, h in enumerate((h0, h0 + 1)):
                    seg = per_head[j] - acum_t[h:h + 1, :]
                    dec = jnp.where(causal, jnp.exp(jnp.minimum(seg, 0.0)), 0.0)
                    ms.append((cb * dec).astype(BF16))
                lhs = jnp.concatenate(ms, axis=1)
                xp = xdt[:, pl_]
                rhs = jnp.concatenate([jnp.where(first_head, xp, 0.0),
                                       jnp.where(first_head, 0.0, xp)], axis=0).astype(BF16)
                y_pair = _dot(lhs, rhs) + y_off[:, pr * LANES:(pr + 1) * LANES]
                ybuf[rows, pl_] = y_pair
        return carry

    lax.fori_loop(0, ts // SSM_CHUNK, chunk_body, 0, unroll=True)

    y = ybuf[...] + xs * dskip_ref[...]
    y = y * _silu(z)
    parts = []
    for g in range(SSM_GROUPS):
        yg = y[:, g * GROUP_W:(g + 1) * GROUP_W]
        ms = jnp.mean(yg * yg, axis=-1, keepdims=True)
        parts.append(yg * lax.rsqrt(ms + RMS_EPS))
    y = jnp.concatenate(parts, axis=1) * ssnw_ref[...]

    ycat = jnp.concatenate([ysc.astype(BF16), y.astype(BF16)], axis=1)
    mix = _dot(ycat, wout_ref[...])
    out_ref[0] = _layer_norm(alpha * x + (1.0 + gate1) * mix, lng_ref[...], lnb_ref[...])


def _const_spec(shape):
    nd = len(shape)
    return pl.BlockSpec(shape, lambda b, s: (0,) * nd, pipeline_mode=pl.Buffered(1))


def _mixer_call(x, cond3, weights, alpha):
    bsz, seqlen, d = x.shape
    ts = MIX_TS
    kern = functools.partial(_mixer_kernel, alpha, ts)
    in_specs = [pl.BlockSpec((1, ts, d), lambda b, s: (b, s, 0)),
                pl.BlockSpec((1, 1, cond3.shape[2]), lambda b, s: (b, 0, 0))]
    in_specs += [_const_spec(w.shape) for w in weights]
    return pl.pallas_call(
        kern,
        grid=(bsz, seqlen // ts),
        in_specs=in_specs,
        out_specs=pl.BlockSpec((1, ts, d), lambda b, s: (b, s, 0)),
        out_shape=jax.ShapeDtypeStruct((bsz, seqlen, d), F32),
        scratch_shapes=[pltpu.VMEM((ts + CARRY, D_MODEL), F32),
                        pltpu.VMEM((ts + CARRY, SSM_CONV_DIM), F32),
                        pltpu.VMEM((SSM_GROUPS, SSM_STATE, GROUP_W), F32),
                        pltpu.VMEM((ts, D_MODEL), F32),
                        pltpu.VMEM((ts, 2 * SSM_GN), F32),
                        pltpu.VMEM((ts, LANES), F32),
                        pltpu.VMEM((ts, D_MODEL), F32)],
        compiler_params=pltpu.CompilerParams(
            dimension_semantics=("arbitrary", "arbitrary"), vmem_limit_bytes=VMEM_LIMIT),
        name="mixer",
    )(x, cond3, *weights)


def _sort_network(n):
    pairs = []

    def merge(lo, m, r):
        step = 2 * r
        if step < m:
            merge(lo, m, step)
            merge(lo + r, m, step)
            pairs.extend((i, i + r) for i in range(lo + r, lo + m - r, step))
        else:
            pairs.append((lo, lo + r))

    def sort(lo, m):
        if m > 1:
            sort(lo, m // 2)
            sort(lo + m // 2, m // 2)
            merge(lo, m, 1)

    sort(0, n)
    return pairs


_SORT16 = _sort_network(PEER_TOPK)


def _top_values(slabs):
    v = list(slabs)
    for i, j in _SORT16:
        v[i], v[j] = jnp.maximum(v[i], v[j]), jnp.minimum(v[i], v[j])
    vals = []
    for k in range(PEER_TOPK):
        mx = jnp.max(v[0], axis=0, keepdims=True)
        vals.append(mx)
        if k + 1 < PEER_TOPK:
            hit = v[0] == mx
            for r in range(PEER_TOPK - 1 - k):
                v[r] = jnp.where(hit, v[r + 1], v[r])
    return vals


def _slabs(x):
    return [x[r * SUBLANES:(r + 1) * SUBLANES] for r in range(x.shape[0] // SUBLANES)]


def _stack_rows(rows, row_idx):
    out = rows[-1]
    for r in range(len(rows) - 2, -1, -1):
        out = jnp.where(row_idx == r, rows[r], out)
    return out


def _prep_kernel(x1_ref, cond_ref, wqt_ref, k1_ref, k2_ref, u2t_ref, rank2_ref, e2_ref, l1_ref, e1_ref):
    x1 = x1_ref[...]
    cond = cond_ref[0]
    shift2 = cond[:, 3 * D_MODEL:4 * D_MODEL]
    scale2 = cond[:, 4 * D_MODEL:5 * D_MODEL]
    u2t = (x1 * (1.0 + scale2) + shift2).T.astype(BF16)
    u2t_ref[...] = u2t
    q_t = _dot(wqt_ref[...], u2t)
    row16 = lax.broadcasted_iota(jnp.int32, (PEER_TOPK, 1), 0)
    for h in range(PEER_HEADS):
        q1 = q_t[h * PEER_DKEY:h * PEER_DKEY + PEER_HALF].astype(BF16)
        q2 = q_t[h * PEER_DKEY + PEER_HALF:(h + 1) * PEER_DKEY].astype(BF16)
        s1 = _dot(k1_ref[h], q1)
        s2 = _dot(k2_ref[h], q2)
        a = _top_values(_slabs(s1))
        b = _top_values(_slabs(s2))
        a_st = _stack_rows(a, row16)
        b_st = _stack_rows(b, row16)
        pieces = [a[r] + b_st for r in range(4)]
        pieces += [jnp.where(row16 >= 4, b[c] + a_st, NEG_INF) for c in range(3)]
        cand = [sl for pc in pieces for sl in _slabs(pc)]
        cand += [jnp.full(cand[0].shape, NEG_INF, F32)] * (PEER_TOPK - len(cand))
        tau = _top_values(cand)[PEER_TOPK - 1]
        rank2 = jnp.full(s2.shape, float(PEER_TOPK), F32)
        for r in range(PEER_TOPK):
            rank2 = jnp.where(s2 == b[r], float(r), rank2)
        m = a[0] + b[0]
        zsum = None
        for pc in pieces:
            e = jnp.sum(jnp.where(pc >= tau, jnp.exp(pc - m), 0.0), axis=0, keepdims=True)
            zsum = e if zsum is None else zsum + e
        lmap = jnp.zeros(s1.shape, F32)
        for r in range(PEER_TOPK):
            cnt = jnp.sum(jnp.where(a[r] + b_st >= tau, 1.0, 0.0), axis=0, keepdims=True)
            lmap = jnp.where(s1 == a[r], cnt, lmap)
        rank2_ref[h] = rank2.astype(BF16)
        e2_ref[h] = jnp.exp(s2 - b[0]).astype(BF16)
        l1_ref[h] = lmap
        e1_ref[h] = jnp.exp(s1 - a[0]) * (0.5 / zsum)


def _prep_call(x1_flat, cond3, wqt, k1, k2, seqlen):
    t, d = x1_flat.shape
    tm = PREP_TM
    per_b = seqlen // tm
    score_spec = pl.BlockSpec((PEER_HEADS, PEER_NKEYS, tm), lambda i: (0, 0, i))
    return pl.pallas_call(
        _prep_kernel,
        grid=(t // tm,),
        in_specs=[pl.BlockSpec((tm, d), lambda i: (i, 0)),
                  pl.BlockSpec((1, 1, cond3.shape[2]), lambda i: (i // per_b, 0, 0)),
                  pl.BlockSpec(wqt.shape, lambda i: (0, 0), pipeline_mode=pl.Buffered(1)),
                  pl.BlockSpec(k1.shape, lambda i: (0, 0, 0), pipeline_mode=pl.Buffered(1)),
                  pl.BlockSpec(k2.shape, lambda i: (0, 0, 0), pipeline_mode=pl.Buffered(1))],
        out_specs=[pl.BlockSpec((d, tm), lambda i: (0, i)),
                   score_spec, score_spec, score_spec, score_spec],
        out_shape=[jax.ShapeDtypeStruct((d, t), BF16),
                   jax.ShapeDtypeStruct((PEER_HEADS, PEER_NKEYS, t), BF16),
                   jax.ShapeDtypeStruct((PEER_HEADS, PEER_NKEYS, t), BF16),
                   jax.ShapeDtypeStruct((PEER_HEADS, PEER_NKEYS, t), F32),
                   jax.ShapeDtypeStruct((PEER_HEADS, PEER_NKEYS, t), F32)],
        compiler_params=pltpu.CompilerParams(
            dimension_semantics=("arbitrary",), vmem_limit_bytes=VMEM_LIMIT),
        name="peer_prep",
    )(x1_flat, cond3, wqt, k1, k2)


def _peer_kernel(alpha, n_sub, sub_i,
                 u2t_ref, rank2_ref, e2_ref, l1_ref, e1_ref, u_ref, vt_ref, x1_ref,
                 cond_ref, lng_ref, lnb_ref, out_ref,
                 acc_ref, h_even, h_odd, w_even, w_odd, act_even, act_odd):
    e_idx = pl.program_id(1)

    @pl.when(e_idx == 0)
    def _():
        acc_ref[...] = jnp.zeros(acc_ref.shape, F32)

    sub = sub_i * PEER_NKEYS
    h_bufs, w_bufs, act_bufs = (h_even, h_odd), (w_even, w_odd), (act_even, act_odd)

    def first_matmul(k):
        h_bufs[k % 2][...] = _dot(u_ref[k * sub:(k + 1) * sub, :], u2t_ref[...])

    def gates(k):
        for ii in range(sub_i):
            i = e_idx * (n_sub * sub_i) + k * sub_i + ii
            w = None
            for h in range(PEER_HEADS):
                lrow = l1_ref[h, pl.ds(i, 1), :].astype(BF16)
                erow = e1_ref[h, pl.ds(i, 1), :].astype(BF16)
                wh = jnp.where(rank2_ref[h] < lrow, e2_ref[h], 0.0) * erow
                w = wh if w is None else w + wh
            w_bufs[k % 2][ii * PEER_NKEYS:(ii + 1) * PEER_NKEYS, :] = w

    def activate(k):
        hh = h_bufs[k % 2][...]
        gelu2 = hh * (1.0 + lax.erf(hh * 0.7071067811865476))
        act_bufs[k % 2][...] = gelu2.astype(BF16) * w_bufs[k % 2][...]

    def second_matmul(k):
        acc_ref[...] += _dot(vt_ref[k], act_bufs[k % 2][...])

    first_matmul(0)
    gates(0)
    for k in range(n_sub):
        if k + 1 < n_sub:
            first_matmul(k + 1)
            gates(k + 1)
        activate(k)
        if k >= 1:
            second_matmul(k - 1)
    second_matmul(n_sub - 1)

    @pl.when(e_idx == pl.num_programs(1) - 1)
    def _():
        ffn = acc_ref[...].T
        gate2 = cond_ref[0][:, 5 * D_MODEL:6 * D_MODEL]
        out_ref[...] = _layer_norm(alpha * x1_ref[...] + (1.0 + gate2) * ffn,
                                   lng_ref[...], lnb_ref[...])


def _peer_call(u2t, rank2, e2, l1, e1, u_bf, vt_bf, x1_flat, cond3, ln_g, ln_b, seqlen, alpha):
    d, t = u2t.shape
    n_exp = u_bf.shape[0]
    tm, te = PEER_TM, PEER_TE
    per_b = seqlen // tm
    sub = PEER_SUB_I * PEER_NKEYS
    n_sub = te // sub
    assert vt_bf.shape == (n_exp // sub, d, sub)
    kern = functools.partial(_peer_kernel, alpha, n_sub, PEER_SUB_I)
    score_spec = pl.BlockSpec((PEER_HEADS, PEER_NKEYS, tm), lambda i, e: (0, 0, i))
    stage_f32 = pltpu.VMEM((sub, tm), F32)
    stage_bf16 = pltpu.VMEM((sub, tm), BF16)
    return pl.pallas_call(
        kern,
        grid=(t // tm, n_exp // te),
        in_specs=[pl.BlockSpec((d, tm), lambda i, e: (0, i)),
                  score_spec, score_spec, score_spec, score_spec,
                  pl.BlockSpec((te, d), lambda i, e: (e, 0)),
                  pl.BlockSpec((n_sub, d, sub), lambda i, e: (e, 0, 0)),
                  pl.BlockSpec((tm, d), lambda i, e: (i, 0)),
                  pl.BlockSpec((1, 1, cond3.shape[2]), lambda i, e: (i // per_b, 0, 0)),
                  pl.BlockSpec((1, d), lambda i, e: (0, 0)),
                  pl.BlockSpec((1, d), lambda i, e: (0, 0))],
        out_specs=pl.BlockSpec((tm, d), lambda i, e: (i, 0)),
        out_shape=jax.ShapeDtypeStruct((t, d), F32),
        scratch_shapes=[pltpu.VMEM((d, tm), F32), stage_f32, stage_f32,
                        stage_bf16, stage_bf16, stage_bf16, stage_bf16],
        compiler_params=pltpu.CompilerParams(
            dimension_semantics=("arbitrary", "arbitrary"), vmem_limit_bytes=VMEM_LIMIT),
        name="peer",
    )(u2t, rank2, e2, l1, e1, u_bf, vt_bf, x1_flat, cond3, ln_g, ln_b)


def _mixer_constants():
    ch = jnp.arange(D_MODEL)
    hd = jnp.arange(LANES)
    g16 = (ch[:, None] // HEAD_DIM == hd[None, :]).astype(BF16)
    e16 = (hd[:, None] == ch[None, :] // HEAD_DIM).astype(BF16)
    tt = jnp.arange(SSM_CHUNK)
    tri = (tt[:, None] >= tt[None, :]).astype(BF16)
    return g16, e16, tri


def _row(v):
    return v.reshape(1, -1).astype(F32)


def _pad_lanes(v):
    v = v.reshape(1, -1).astype(F32)
    return jnp.pad(v, ((0, 0), (0, LANES - v.shape[1])))


def _layer(x, cond3, alpha, w_in, sc_conv_w, ssm_conv_w, ssm_conv_b, dt_bias, a_log, d_skip,
           sc_norm_w, ssm_norm_w, w_out, ln1_g, ln1_b, w_query, sub_keys1, sub_keys2,
           expert_u, expert_v, ln2_g, ln2_b):
    bsz, seqlen, d = x.shape
    o_z = 3 * D_MODEL
    o_xbc = o_z + D_MODEL
    o_dt = o_xbc + SSM_CONV_DIM
    w_dt = jnp.pad(w_in[:, o_dt:], ((0, 0), (0, LANES - SSM_HEADS)))
    weights = [w_in[:, :o_z].astype(BF16), w_in[:, o_z:o_xbc].astype(BF16),
               w_in[:, o_xbc:o_dt].astype(BF16), w_dt.astype(BF16),
               sc_conv_w.astype(F32), ssm_conv_w.astype(F32), _row(ssm_conv_b),
               _pad_lanes(dt_bias), _pad_lanes(a_log), _row(jnp.repeat(d_skip, HEAD_DIM)),
               _row(sc_norm_w), _row(ssm_norm_w), w_out.astype(BF16), _row(ln1_g), _row(ln1_b),
               *_mixer_constants()]
    x1 = _mixer_call(x, cond3, weights, alpha)
    x1_flat = x1.reshape(bsz * seqlen, d)
    u2t, rank2, e2, l1, e1 = _prep_call(x1_flat, cond3, w_query.T.astype(BF16),
                                        sub_keys1.astype(BF16), sub_keys2.astype(BF16), seqlen)
    sub = PEER_SUB_I * PEER_NKEYS
    vt = expert_v.reshape(-1, sub, d).transpose(0, 2, 1).astype(BF16)
    out = _peer_call(u2t, rank2, e2, l1, e1, expert_u.astype(BF16), vt,
                     x1_flat, cond3, _row(ln2_g), _row(ln2_b), seqlen, alpha)
    return out.reshape(bsz, seqlen, d)


def kernel(x, c, w_cond, b_cond, w_in, sc_conv_w, ssm_conv_w, ssm_conv_b, dt_bias, a_log, d_skip,
           sc_norm_w, ssm_norm_w, w_out, ln1_g, ln1_b, w_query, sub_keys1, sub_keys2, expert_u,
           expert_v, ln2_g, ln2_b):
    depth = w_cond.shape[0]
    bsz = x.shape[0]
    alpha = (2.0 * depth) ** 0.25
    c_pad = jnp.pad(c, ((0, (-bsz) % CARRY), (0, 0)))
    for l in range(depth):
        cond = _cond_call(c_pad, w_cond[l], b_cond[l].reshape(1, -1))[:bsz]
        cond3 = cond.reshape(bsz, 1, -1)
        x = _layer(x, cond3, alpha, w_in[l], sc_conv_w[l], ssm_conv_w[l], ssm_conv_b[l], dt_bias[l],
                   a_log[l], d_skip[l], sc_norm_w[l], ssm_norm_w[l], w_out[l], ln1_g[l], ln1_b[l],
                   w_query[l], sub_keys1[l], sub_keys2[l], expert_u[l], expert_v[l], ln2_g[l],
                   ln2_b[l])
    return x
```

```python
import functools

import jax
import jax.numpy as jnp
from jax import lax
from jax.experimental import pallas as pl
from jax.experimental.pallas import tpu as pltpu

F32 = jnp.float32
BF16 = jnp.bfloat16

D_MODEL = 1024
HEAD_DIM = 64
SC_HEADS = 16
SC_WIN = 3
SSM_HEADS = 16
SSM_GROUPS = 2
SSM_STATE = 128
SSM_CONV = 4
SSM_CHUNK = 128
SSM_GN = SSM_GROUPS * SSM_STATE
SSM_CONV_DIM = D_MODEL + 2 * SSM_GN
GROUP_W = D_MODEL // SSM_GROUPS
PEER_HEADS = 8
PEER_NKEYS = 128
PEER_TOPK = 16
PEER_HALF = 128
PEER_DKEY = 256
LN_EPS = 1e-5
RMS_EPS = 1e-6
LANES = 128
SUBLANES = 8
CARRY = SUBLANES
NEG_INF = float("-inf")
INV_SQRT2 = 0.7071067811865476

MIX_TS = 512
PREP_TM = 512
PEER_TM = 512
PEER_TE = 2048
PEER_SUB_I = 4
VMEM_LIMIT = 56 * 1024 * 1024


def _dot(a, b):
    return jnp.dot(a, b, preferred_element_type=F32)


def _dot_nt(a, b):
    return lax.dot_general(a, b, (((1,), (1,)), ((), ())), preferred_element_type=F32)


def _dot_tn(a, b):
    return lax.dot_general(a, b, (((0,), (0,)), ((), ())), preferred_element_type=F32)


def _split_dot_lhs(a, m, terms):
    out = None
    r = a
    for _ in range(terms):
        hi = r.astype(BF16)
        part = _dot(hi, m)
        out = part if out is None else out + part
        r = r - hi.astype(F32)
    return out


def _split_dot_rhs(m, a, terms):
    out = None
    r = a
    for _ in range(terms):
        hi = r.astype(BF16)
        part = _dot(m, hi)
        out = part if out is None else out + part
        r = r - hi.astype(F32)
    return out


def _sigmoid(x):
    return 1.0 / (1.0 + jnp.exp(-x))


def _silu(x):
    return x * _sigmoid(x)


def _softplus(x):
    return jnp.maximum(x, 0.0) + jnp.log(1.0 + jnp.exp(-jnp.abs(x)))


def _layer_norm(v, g, b):
    mu = jnp.mean(v, axis=-1, keepdims=True)
    d = v - mu
    var = jnp.mean(d * d, axis=-1, keepdims=True)
    return d * lax.rsqrt(var + LN_EPS) * g + b


def _cond_kernel(c_ref, w_ref, b_ref, o_ref):
    c = c_ref[...]
    o_ref[...] = jnp.dot(_silu(c), w_ref[...], preferred_element_type=F32,
                         precision=lax.Precision.HIGHEST) + b_ref[...]


def _cond_call(c_pad, w_cond, b_cond):
    rows, d = c_pad.shape
    n = w_cond.shape[1]
    bn = 1024
    return pl.pallas_call(
        _cond_kernel,
        grid=(n // bn,),
        in_specs=[pl.BlockSpec((rows, d), lambda j: (0, 0)),
                  pl.BlockSpec((d, bn), lambda j: (0, j)),
                  pl.BlockSpec((1, bn), lambda j: (0, j))],
        out_specs=pl.BlockSpec((rows, bn), lambda j: (0, j)),
        out_shape=jax.ShapeDtypeStruct((rows, n), F32),
        name="cond",
    )(c_pad, w_cond, b_cond)


def _mixer_kernel(alpha, ts,
                  x_ref, cond_ref, wsc_ref, wz_ref, wxbc_ref, wdt_ref, scw_ref, xcw_ref, xcb_ref,
                  dtb_ref, alog_ref, dskip_ref, scnw_ref, ssnw_ref, wout_ref, lng_ref, lnb_ref,
                  g16_ref, e16_ref, tri_ref,
                  out_ref,
                  pbuf, xbuf, state, xdt_s, bc_s, adt_s, ybuf):
    s_idx = pl.program_id(1)

    @pl.when(s_idx == 0)
    def _():
        pbuf[0:CARRY, :] = jnp.zeros((CARRY, D_MODEL), F32)
        xbuf[0:CARRY, :] = jnp.zeros((CARRY, SSM_CONV_DIM), F32)
        state[...] = jnp.zeros(state.shape, F32)

    x = x_ref[0]
    cond = cond_ref[0]
    shift1 = cond[:, 0:D_MODEL]
    scale1 = cond[:, D_MODEL:2 * D_MODEL]
    gate1 = cond[:, 2 * D_MODEL:3 * D_MODEL]
    u1 = (x * (1.0 + scale1) + shift1).astype(BF16)

    sc = _dot(u1, wsc_ref[...])
    p = sc[:, D_MODEL:2 * D_MODEL] * sc[:, 2 * D_MODEL:3 * D_MODEL]
    pbuf[CARRY:CARRY + ts, :] = p
    conv = p * scw_ref[SC_WIN - 1:SC_WIN, :]
    for k in range(SC_WIN - 1):
        off = CARRY - (SC_WIN - 1) + k
        conv = conv + pbuf[off:off + ts, :] * scw_ref[k:k + 1, :]
    pbuf[0:CARRY, :] = pbuf[ts:ts + CARRY, :]
    ysc = sc[:, 0:D_MODEL] * conv
    ss = _dot((ysc * ysc).astype(BF16), g16_ref[...])
    r = lax.rsqrt(ss * (1.0 / HEAD_DIM) + RMS_EPS)
    ysc = ysc * _split_dot_lhs(r, e16_ref[...], 2) * scnw_ref[...]

    z = _dot(u1, wz_ref[...])
    xbuf[CARRY:CARRY + ts, :] = _dot(u1, wxbc_ref[...])
    xc = xcb_ref[...] + xbuf[CARRY:CARRY + ts, :] * xcw_ref[SSM_CONV - 1:SSM_CONV, :]
    for k in range(SSM_CONV - 1):
        off = CARRY - (SSM_CONV - 1) + k
        xc = xc + xbuf[off:off + ts, :] * xcw_ref[k:k + 1, :]
    xbuf[0:CARRY, :] = xbuf[ts:ts + CARRY, :]
    xc = _silu(xc)
    xs = xc[:, 0:D_MODEL]
    bc_s[...] = xc[:, D_MODEL:SSM_CONV_DIM]

    lane = lax.broadcasted_iota(jnp.int32, (1, LANES), 1)
    head_lane = lane < SSM_HEADS
    dt = jnp.where(head_lane, _softplus(_dot(u1, wdt_ref[...]) + dtb_ref[...]), 0.0)
    a_neg = jnp.where(head_lane, -jnp.exp(alog_ref[...]), 0.0)
    adt_s[...] = dt * a_neg
    xdt_s[...] = xs * _split_dot_lhs(dt, e16_ref[...], 2)

    row_i = lax.broadcasted_iota(jnp.int32, (SSM_CHUNK, SSM_CHUNK), 0)
    col_i = lax.broadcasted_iota(jnp.int32, (SSM_CHUNK, SSM_CHUNK), 1)
    causal = row_i >= col_i
    lane_p = lax.broadcasted_iota(jnp.int32, (SSM_CHUNK, LANES), 1)
    first_head = lane_p < HEAD_DIM

    def chunk_body(c, carry):
        r0 = pl.multiple_of(c * SSM_CHUNK, SSM_CHUNK)
        rows = pl.ds(r0, SSM_CHUNK)
        adt = adt_s[rows, :]
        acum = _split_dot_rhs(tri_ref[...], adt, 3)
        acum_t = acum.T
        acum_full = _split_dot_lhs(acum, e16_ref[...], 2)
        alast_full = acum_full[SSM_CHUNK - 1:SSM_CHUNK, :]
        dfs_full = jnp.exp(acum_full)
        dte_full = jnp.exp(alast_full - acum_full)
        cdec_full = jnp.exp(alast_full)
        xdt = xdt_s[rows, :]
        for g in range(SSM_GROUPS):
            gl = slice(g * GROUP_W, (g + 1) * GROUP_W)
            b_g = bc_s[rows, g * SSM_STATE:(g + 1) * SSM_STATE].astype(BF16)
            c_g = bc_s[rows, SSM_GN + g * SSM_STATE:SSM_GN + (g + 1) * SSM_STATE].astype(BF16)
            cb = _dot_nt(c_g, b_g)
            st = state[g]
            y_off = _dot(c_g, st.astype(BF16)) * dfs_full[:, gl]
            xw = (xdt[:, gl] * dte_full[:, gl]).astype(BF16)
            state[g] = st * cdec_full[:, gl] + _dot_tn(b_g, xw)
            for pr in range(GROUP_W // LANES):
                h0 = g * (SSM_HEADS // SSM_GROUPS) + 2 * pr
                pl_ = slice(g * GROUP_W + pr * LANES, g * GROUP_W + (pr + 1) * LANES)
                both = acum_full[:, pl_]
                swapped = pltpu.roll(both, HEAD_DIM, 1)
                per_head = (jnp.where(first_head, both, swapped), jnp.where(first_head, swapped, both))
                ms = []
                for j, h in enumerate((h0, h0 + 1)):
                    seg = per_head[j] - acum_t[h:h + 1, :]
                    dec = jnp.where(causal, jnp.exp(jnp.minimum(seg, 0.0)), 0.0)
                    ms.append((cb * dec).astype(BF16))
                lhs = jnp.concatenate(ms, axis=1)
                xp = xdt[:, pl_]
                rhs = jnp.concatenate([jnp.where(first_head, xp, 0.0),
                                       jnp.where(first_head, 0.0, xp)], axis=0).astype(BF16)
                y_pair = _dot(lhs, rhs) + y_off[:, pr * LANES:(pr + 1) * LANES]
                ybuf[rows, pl_] = y_pair
        return carry

    lax.fori_loop(0, ts // SSM_CHUNK, chunk_body, 0, unroll=True)

    y = ybuf[...] + xs * dskip_ref[...]
    y = y * _silu(z)
    parts = []
    for g in range(SSM_GROUPS):
        yg = y[:, g * GROUP_W:(g + 1) * GROUP_W]
        ms = jnp.mean(yg * yg, axis=-1, keepdims=True)
        parts.append(yg * lax.rsqrt(ms + RMS_EPS))
    y = jnp.concatenate(parts, axis=1) * ssnw_ref[...]

    ycat = jnp.concatenate([ysc.astype(BF16), y.astype(BF16)], axis=1)
    mix = _dot(ycat, wout_ref[...])
    out_ref[0] = _layer_norm(alpha * x + (1.0 + gate1) * mix, lng_ref[...], lnb_ref[...])


def _const_spec(shape):
    nd = len(shape)
    return pl.BlockSpec(shape, lambda b, s: (0,) * nd, pipeline_mode=pl.Buffered(1))


def _mixer_call(x, cond3, weights, alpha):
    bsz, seqlen, d = x.shape
    ts = MIX_TS
    kern = functools.partial(_mixer_kernel, alpha, ts)
    in_specs = [pl.BlockSpec((1, ts, d), lambda b, s: (b, s, 0)),
                pl.BlockSpec((1, 1, cond3.shape[2]), lambda b, s: (b, 0, 0))]
    in_specs += [_const_spec(w.shape) for w in weights]
    return pl.pallas_call(
        kern,
        grid=(bsz, seqlen // ts),
        in_specs=in_specs,
        out_specs=pl.BlockSpec((1, ts, d), lambda b, s: (b, s, 0)),
        out_shape=jax.ShapeDtypeStruct((bsz, seqlen, d), F32),
        scratch_shapes=[pltpu.VMEM((ts + CARRY, D_MODEL), F32),
                        pltpu.VMEM((ts + CARRY, SSM_CONV_DIM), F32),
                        pltpu.VMEM((SSM_GROUPS, SSM_STATE, GROUP_W), F32),
                        pltpu.VMEM((ts, D_MODEL), F32),
                        pltpu.VMEM((ts, 2 * SSM_GN), F32),
                        pltpu.VMEM((ts, LANES), F32),
                        pltpu.VMEM((ts, D_MODEL), F32)],
        compiler_params=pltpu.CompilerParams(
            dimension_semantics=("arbitrary", "arbitrary"), vmem_limit_bytes=VMEM_LIMIT),
        name="mixer",
    )(x, cond3, *weights)


def _sort_network(n):
    pairs = []

    def merge(lo, m, r):
        step = 2 * r
        if step < m:
            merge(lo, m, step)
            merge(lo + r, m, step)
            pairs.extend((i, i + r) for i in range(lo + r, lo + m - r, step))
        else:
            pairs.append((lo, lo + r))

    def sort(lo, m):
        if m > 1:
            sort(lo, m // 2)
            sort(lo + m // 2, m // 2)
            merge(lo, m, 1)

    sort(0, n)
    return pairs


_SORT16 = _sort_network(PEER_TOPK)
_L_FULL_ROWS = 3


def _top_values(slabs):
    v = list(slabs)
    for i, j in _SORT16:
        v[i], v[j] = jnp.maximum(v[i], v[j]), jnp.minimum(v[i], v[j])
    vals = []
    for k in range(PEER_TOPK):
        mx = jnp.max(v[0], axis=0, keepdims=True)
        vals.append(mx)
        if k + 1 < PEER_TOPK:
            hit = v[0] == mx
            for r in range(PEER_TOPK - 1 - k):
                v[r] = jnp.where(hit, v[r + 1], v[r])
    return vals


def _slabs(x):
    return [x[r * SUBLANES:(r + 1) * SUBLANES] for r in range(x.shape[0] // SUBLANES)]


def _stack_rows(rows, row_idx):
    out = rows[-1]
    for r in range(len(rows) - 2, -1, -1):
        out = jnp.where(row_idx == r, rows[r], out)
    return out


def _prep_kernel(x1_ref, cond_ref, wqt_ref, k1_ref, k2_ref, u2t_ref, rank2_ref, e2_ref, l1_ref, e1_ref):
    x1 = x1_ref[...]
    cond = cond_ref[0]
    shift2 = cond[:, 3 * D_MODEL:4 * D_MODEL]
    scale2 = cond[:, 4 * D_MODEL:5 * D_MODEL]
    u2t = (x1 * (1.0 + scale2) + shift2).T.astype(BF16)
    u2t_ref[...] = u2t
    q_t = _dot(wqt_ref[...], u2t)
    row16 = lax.broadcasted_iota(jnp.int32, (PEER_TOPK, 1), 0)
    for h in range(PEER_HEADS):
        q1 = q_t[h * PEER_DKEY:h * PEER_DKEY + PEER_HALF].astype(BF16)
        q2 = q_t[h * PEER_DKEY + PEER_HALF:(h + 1) * PEER_DKEY].astype(BF16)
        s1 = _dot(k1_ref[h], q1)
        s2 = _dot(k2_ref[h], q2)
        a = _top_values(_slabs(s1))
        b = _top_values(_slabs(s2))
        a_st = _stack_rows(a, row16)
        b_st = _stack_rows(b, row16)
        pieces = [a[r] + b_st for r in range(4)]
        pieces += [jnp.where(row16 >= 4, b[c] + a_st, NEG_INF) for c in range(3)]
        cand = [sl for pc in pieces for sl in _slabs(pc)]
        cand += [jnp.full(cand[0].shape, NEG_INF, F32)] * (PEER_TOPK - len(cand))
        tau = _top_values(cand)[PEER_TOPK - 1]
        rank2 = jnp.full(s2.shape, float(PEER_TOPK), F32)
        for r in range(PEER_TOPK):
            rank2 = jnp.where(s2 == b[r], float(r), rank2)
        m = a[0] + b[0]
        zsum = None
        for pc in pieces:
            e = jnp.sum(jnp.where(pc >= tau, jnp.exp(pc - m), 0.0), axis=0, keepdims=True)
            zsum = e if zsum is None else zsum + e
        lmap = jnp.zeros(s1.shape, F32)
        for c in range(PEER_TOPK // (_L_FULL_ROWS + 1)):
            lmap = jnp.where(s1 + b[c] >= tau, lmap + 1.0, lmap)
        for r in range(_L_FULL_ROWS):
            cnt = jnp.sum(jnp.where(a[r] + b_st >= tau, 1.0, 0.0), axis=0, keepdims=True)
            lmap = jnp.where(s1 == a[r], cnt, lmap)
        rank2_ref[h] = rank2.astype(BF16)
        e2_ref[h] = jnp.exp(s2 - b[0]).astype(BF16)
        l1_ref[h] = lmap
        e1_ref[h] = jnp.exp(s1 - a[0]) * (0.5 / zsum)


def _prep_call(x1_flat, cond3, wqt, k1, k2, seqlen):
    t, d = x1_flat.shape
    tm = PREP_TM
    per_b = seqlen // tm
    score_spec = pl.BlockSpec((PEER_HEADS, PEER_NKEYS, tm), lambda i: (0, 0, i))
    return pl.pallas_call(
        _prep_kernel,
        grid=(t // tm,),
        in_specs=[pl.BlockSpec((tm, d), lambda i: (i, 0)),
                  pl.BlockSpec((1, 1, cond3.shape[2]), lambda i: (i // per_b, 0, 0)),
                  pl.BlockSpec(wqt.shape, lambda i: (0, 0), pipeline_mode=pl.Buffered(1)),
                  pl.BlockSpec(k1.shape, lambda i: (0, 0, 0), pipeline_mode=pl.Buffered(1)),
                  pl.BlockSpec(k2.shape, lambda i: (0, 0, 0), pipeline_mode=pl.Buffered(1))],
        out_specs=[pl.BlockSpec((d, tm), lambda i: (0, i)),
                   score_spec, score_spec, score_spec, score_spec],
        out_shape=[jax.ShapeDtypeStruct((d, t), BF16),
                   jax.ShapeDtypeStruct((PEER_HEADS, PEER_NKEYS, t), BF16),
                   jax.ShapeDtypeStruct((PEER_HEADS, PEER_NKEYS, t), BF16),
                   jax.ShapeDtypeStruct((PEER_HEADS, PEER_NKEYS, t), F32),
                   jax.ShapeDtypeStruct((PEER_HEADS, PEER_NKEYS, t), F32)],
        compiler_params=pltpu.CompilerParams(
            dimension_semantics=("arbitrary",), vmem_limit_bytes=VMEM_LIMIT),
        name="peer_prep",
    )(x1_flat, cond3, wqt, k1, k2)


def _peer_kernel(alpha, n_sub, sub_i,
                 u2t_ref, rank2_ref, e2_ref, l1_ref, e1_ref, u_ref, vt_ref, x1_ref,
                 cond_ref, lng_ref, lnb_ref, out_ref,
                 acc_ref, h_even, h_odd, w_even, w_odd, act_even, act_odd):
    e_idx = pl.program_id(1)

    @pl.when(e_idx == 0)
    def _():
        acc_ref[...] = jnp.zeros(acc_ref.shape, F32)

    sub = sub_i * PEER_NKEYS
    h_bufs, w_bufs, act_bufs = (h_even, h_odd), (w_even, w_odd), (act_even, act_odd)

    def first_matmul(k):
        h_bufs[k % 2][...] = _dot(u_ref[k * sub:(k + 1) * sub, :], u2t_ref[...])

    def gates(k):
        for ii in range(sub_i):
            i = e_idx * (n_sub * sub_i) + k * sub_i + ii
            w = None
            for h in range(PEER_HEADS):
                lrow = l1_ref[h, pl.ds(i, 1), :].astype(BF16)
                erow = e1_ref[h, pl.ds(i, 1), :].astype(BF16)
                wh = jnp.where(rank2_ref[h] < lrow, e2_ref[h], 0.0) * erow
                w = wh if w is None else w + wh
            w_bufs[k % 2][ii * PEER_NKEYS:(ii + 1) * PEER_NKEYS, :] = w

    def activate(k):
        hh = h_bufs[k % 2][...]
        gelu2 = hh * (1.0 + lax.erf(hh * INV_SQRT2))
        act_bufs[k % 2][...] = gelu2.astype(BF16) * w_bufs[k % 2][...]

    def second_matmul(k):
        acc_ref[...] += _dot(vt_ref[k], act_bufs[k % 2][...])

    first_matmul(0)
    gates(0)
    for k in range(n_sub):
        if k + 1 < n_sub:
            first_matmul(k + 1)
            gates(k + 1)
        activate(k)
        if k >= 1:
            second_matmul(k - 1)
    second_matmul(n_sub - 1)

    @pl.when(e_idx == pl.num_programs(1) - 1)
    def _():
        ffn = acc_ref[...].T
        gate2 = cond_ref[0][:, 5 * D_MODEL:6 * D_MODEL]
        out_ref[...] = _layer_norm(alpha * x1_ref[...] + (1.0 + gate2) * ffn,
                                   lng_ref[...], lnb_ref[...])


def _peer_call(u2t, rank2, e2, l1, e1, u_bf, vt_bf, x1_flat, cond3, ln_g, ln_b, seqlen, alpha):
    d, t = u2t.shape
    n_exp = u_bf.shape[0]
    tm, te = PEER_TM, PEER_TE
    per_b = seqlen // tm
    sub = PEER_SUB_I * PEER_NKEYS
    n_sub = te // sub
    assert vt_bf.shape == (n_exp // sub, d, sub)
    kern = functools.partial(_peer_kernel, alpha, n_sub, PEER_SUB_I)
    score_spec = pl.BlockSpec((PEER_HEADS, PEER_NKEYS, tm), lambda i, e: (0, 0, i))
    stage_f32 = pltpu.VMEM((sub, tm), F32)
    stage_bf16 = pltpu.VMEM((sub, tm), BF16)
    return pl.pallas_call(
        kern,
        grid=(t // tm, n_exp // te),
        in_specs=[pl.BlockSpec((d, tm), lambda i, e: (0, i)),
                  score_spec, score_spec, score_spec, score_spec,
                  pl.BlockSpec((te, d), lambda i, e: (e, 0)),
                  pl.BlockSpec((n_sub, d, sub), lambda i, e: (e, 0, 0)),
                  pl.BlockSpec((tm, d), lambda i, e: (i, 0)),
                  pl.BlockSpec((1, 1, cond3.shape[2]), lambda i, e: (i // per_b, 0, 0)),
                  pl.BlockSpec((1, d), lambda i, e: (0, 0)),
                  pl.BlockSpec((1, d), lambda i, e: (0, 0))],
        out_specs=pl.BlockSpec((tm, d), lambda i, e: (i, 0)),
        out_shape=jax.ShapeDtypeStruct((t, d), F32),
        scratch_shapes=[pltpu.VMEM((d, tm), F32), stage_f32, stage_f32,
                        stage_bf16, stage_bf16, stage_bf16, stage_bf16],
        compiler_params=pltpu.CompilerParams(
            dimension_semantics=("arbitrary", "arbitrary"), vmem_limit_bytes=VMEM_LIMIT),
        name="peer",
    )(u2t, rank2, e2, l1, e1, u_bf, vt_bf, x1_flat, cond3, ln_g, ln_b)


def _mixer_constants():
    ch = jnp.arange(D_MODEL)
    hd = jnp.arange(LANES)
    g16 = (ch[:, None] // HEAD_DIM == hd[None, :]).astype(BF16)
    e16 = (hd[:, None] == ch[None, :] // HEAD_DIM).astype(BF16)
    tt = jnp.arange(SSM_CHUNK)
    tri = (tt[:, None] >= tt[None, :]).astype(BF16)
    return g16, e16, tri


def _row(v):
    return v.reshape(1, -1).astype(F32)


def _pad_lanes(v):
    v = v.reshape(1, -1).astype(F32)
    return jnp.pad(v, ((0, 0), (0, LANES - v.shape[1])))


def _layer(x, cond3, alpha, w_in, sc_conv_w, ssm_conv_w, ssm_conv_b, dt_bias, a_log, d_skip,
           sc_norm_w, ssm_norm_w, w_out, ln1_g, ln1_b, w_query, sub_keys1, sub_keys2,
           expert_u, expert_v, ln2_g, ln2_b):
    bsz, seqlen, d = x.shape
    o_z = 3 * D_MODEL
    o_xbc = o_z + D_MODEL
    o_dt = o_xbc + SSM_CONV_DIM
    w_dt = jnp.pad(w_in[:, o_dt:], ((0, 0), (0, LANES - SSM_HEADS)))
    weights = [w_in[:, :o_z].astype(BF16), w_in[:, o_z:o_xbc].astype(BF16),
               w_in[:, o_xbc:o_dt].astype(BF16), w_dt.astype(BF16),
               sc_conv_w.astype(F32), ssm_conv_w.astype(F32), _row(ssm_conv_b),
               _pad_lanes(dt_bias), _pad_lanes(a_log), _row(jnp.repeat(d_skip, HEAD_DIM)),
               _row(sc_norm_w), _row(ssm_norm_w), w_out.astype(BF16), _row(ln1_g), _row(ln1_b),
               *_mixer_constants()]
    x1 = _mixer_call(x, cond3, weights, alpha)
    x1_flat = x1.reshape(bsz * seqlen, d)
    u2t, rank2, e2, l1, e1 = _prep_call(x1_flat, cond3, w_query.T.astype(BF16),
                                        sub_keys1.astype(BF16), sub_keys2.astype(BF16), seqlen)
    sub = PEER_SUB_I * PEER_NKEYS
    vt = expert_v.reshape(-1, sub, d).transpose(0, 2, 1).astype(BF16)
    out = _peer_call(u2t, rank2, e2, l1, e1, expert_u.astype(BF16), vt,
                     x1_flat, cond3, _row(ln2_g), _row(ln2_b), seqlen, alpha)
    return out.reshape(bsz, seqlen, d)


def kernel(x, c, w_cond, b_cond, w_in, sc_conv_w, ssm_conv_w, ssm_conv_b, dt_bias, a_log, d_skip,
           sc_norm_w, ssm_norm_w, w_out, ln1_g, ln1_b, w_query, sub_keys1, sub_keys2, expert_u,
           expert_v, ln2_g, ln2_b):
    depth = w_cond.shape[0]
    bsz = x.shape[0]
    alpha = (2.0 * depth) ** 0.25
    c_pad = jnp.pad(c, ((0, (-bsz) % CARRY), (0, 0)))
    for l in range(depth):
        cond = _cond_call(c_pad, w_cond[l], b_cond[l].reshape(1, -1))[:bsz]
        cond3 = cond.reshape(bsz, 1, -1)
        x = _layer(x, cond3, alpha, w_in[l], sc_conv_w[l], ssm_conv_w[l], ssm_conv_b[l], dt_bias[l],
                   a_log[l], d_skip[l], sc_norm_w[l], ssm_norm_w[l], w_out[l], ln1_g[l], ln1_b[l],
                   w_query[l], sub_keys1[l], sub_keys2[l], expert_u[l], expert_v[l], ln2_g[l],
                   ln2_b[l])
    return x
```

```python
import functools

import jax
import jax.numpy as jnp
from jax import lax
from jax.experimental import pallas as pl
from jax.experimental.pallas import tpu as pltpu

F32 = jnp.float32
BF16 = jnp.bfloat16

D_MODEL = 1024
HEAD_DIM = 64
SC_HEADS = 16
SC_WIN = 3
SSM_HEADS = 16
SSM_GROUPS = 2
SSM_STATE = 128
SSM_CONV = 4
SSM_CHUNK = 128
SSM_GN = SSM_GROUPS * SSM_STATE
SSM_CONV_DIM = D_MODEL + 2 * SSM_GN
GROUP_W = D_MODEL // SSM_GROUPS
PEER_HEADS = 8
PEER_NKEYS = 128
PEER_TOPK = 16
PEER_HALF = 128
PEER_DKEY = 256
LN_EPS = 1e-5
RMS_EPS = 1e-6
LANES = 128
SUBLANES = 8
CARRY = SUBLANES
NEG_INF = float("-inf")
INV_SQRT2 = 0.7071067811865476

MIX_TS = 512
PREP_TM = 512
PEER_TM = 512
PEER_TE = 2048
PEER_SUB_I = 4
VMEM_LIMIT = 56 * 1024 * 1024


def _dot(a, b):
    return jnp.dot(a, b, preferred_element_type=F32)


def _dot_nt(a, b):
    return lax.dot_general(a, b, (((1,), (1,)), ((), ())), preferred_element_type=F32)


def _dot_tn(a, b):
    return lax.dot_general(a, b, (((0,), (0,)), ((), ())), preferred_element_type=F32)


def _split_dot_lhs(a, m, terms):
    out = None
    r = a
    for _ in range(terms):
        hi = r.astype(BF16)
        part = _dot(hi, m)
        out = part if out is None else out + part
        r = r - hi.astype(F32)
    return out


def _split_dot_rhs(m, a, terms):
    out = None
    r = a
    for _ in range(terms):
        hi = r.astype(BF16)
        part = _dot(m, hi)
        out = part if out is None else out + part
        r = r - hi.astype(F32)
    return out


def _sigmoid(x):
    return 1.0 / (1.0 + jnp.exp(-x))


def _silu(x):
    return x * _sigmoid(x)


def _softplus(x):
    return jnp.maximum(x, 0.0) + jnp.log(1.0 + jnp.exp(-jnp.abs(x)))


def _layer_norm(v, g, b):
    mu = jnp.mean(v, axis=-1, keepdims=True)
    d = v - mu
    var = jnp.mean(d * d, axis=-1, keepdims=True)
    return d * lax.rsqrt(var + LN_EPS) * g + b


def _cond_kernel(c_ref, w_ref, b_ref, o_ref):
    c = c_ref[...]
    o_ref[...] = jnp.dot(_silu(c), w_ref[...], preferred_element_type=F32,
                         precision=lax.Precision.HIGHEST) + b_ref[...]


def _cond_call(c_pad, w_cond, b_cond):
    rows, d = c_pad.shape
    n = w_cond.shape[1]
    bn = 1024
    return pl.pallas_call(
        _cond_kernel,
        grid=(n // bn,),
        in_specs=[pl.BlockSpec((rows, d), lambda j: (0, 0)),
                  pl.BlockSpec((d, bn), lambda j: (0, j)),
                  pl.BlockSpec((1, bn), lambda j: (0, j))],
        out_specs=pl.BlockSpec((rows, bn), lambda j: (0, j)),
        out_shape=jax.ShapeDtypeStruct((rows, n), F32),
        name="cond",
    )(c_pad, w_cond, b_cond)


def _mixer_kernel(alpha, ts,
                  x_ref, cond_ref, wsc_ref, wz_ref, wxbc_ref, wdt_ref, scw_ref, xcw_ref, xcb_ref,
                  dtb_ref, alog_ref, dskip_ref, scnw_ref, ssnw_ref, wout_ref, lng_ref, lnb_ref,
                  g16_ref, e16_ref, tri_ref,
                  out_ref,
                  pbuf, xbuf, state, xdt_s, bc_s, adt_s, ybuf):
    s_idx = pl.program_id(1)

    @pl.when(s_idx == 0)
    def _():
        pbuf[0:CARRY, :] = jnp.zeros((CARRY, D_MODEL), F32)
        xbuf[0:CARRY, :] = jnp.zeros((CARRY, SSM_CONV_DIM), F32)
        state[...] = jnp.zeros(state.shape, F32)

    x = x_ref[0]
    cond = cond_ref[0]
    shift1 = cond[:, 0:D_MODEL]
    scale1 = cond[:, D_MODEL:2 * D_MODEL]
    gate1 = cond[:, 2 * D_MODEL:3 * D_MODEL]
    u1 = (x * (1.0 + scale1) + shift1).astype(BF16)

    sc = _dot(u1, wsc_ref[...])
    p = sc[:, D_MODEL:2 * D_MODEL] * sc[:, 2 * D_MODEL:3 * D_MODEL]
    pbuf[CARRY:CARRY + ts, :] = p
    conv = p * scw_ref[SC_WIN - 1:SC_WIN, :]
    for k in range(SC_WIN - 1):
        off = CARRY - (SC_WIN - 1) + k
        conv = conv + pbuf[off:off + ts, :] * scw_ref[k:k + 1, :]
    pbuf[0:CARRY, :] = pbuf[ts:ts + CARRY, :]
    ysc = sc[:, 0:D_MODEL] * conv
    ss = _dot((ysc * ysc).astype(BF16), g16_ref[...])
    r = lax.rsqrt(ss * (1.0 / HEAD_DIM) + RMS_EPS)
    ysc = ysc * _split_dot_lhs(r, e16_ref[...], 2) * scnw_ref[...]

    xbuf[CARRY:CARRY + ts, :] = _dot(u1, wxbc_ref[...])
    xc = xcb_ref[...] + xbuf[CARRY:CARRY + ts, :] * xcw_ref[SSM_CONV - 1:SSM_CONV, :]
    for k in range(SSM_CONV - 1):
        off = CARRY - (SSM_CONV - 1) + k
        xc = xc + xbuf[off:off + ts, :] * xcw_ref[k:k + 1, :]
    xbuf[0:CARRY, :] = xbuf[ts:ts + CARRY, :]
    xc = _silu(xc)
    xs = xc[:, 0:D_MODEL]
    bc_s[...] = xc[:, D_MODEL:SSM_CONV_DIM]

    lane = lax.broadcasted_iota(jnp.int32, (1, LANES), 1)
    head_lane = lane < SSM_HEADS
    dt = jnp.where(head_lane, _softplus(_dot(u1, wdt_ref[...]) + dtb_ref[...]), 0.0)
    a_neg = jnp.where(head_lane, -jnp.exp(alog_ref[...]), 0.0)
    adt_s[...] = dt * a_neg
    xdt_s[...] = xs * _split_dot_lhs(dt, e16_ref[...], 2)

    row_i = lax.broadcasted_iota(jnp.int32, (SSM_CHUNK, SSM_CHUNK), 0)
    col_i = lax.broadcasted_iota(jnp.int32, (SSM_CHUNK, SSM_CHUNK), 1)
    causal = row_i >= col_i
    lane_p = lax.broadcasted_iota(jnp.int32, (SSM_CHUNK, LANES), 1)
    first_head = lane_p < HEAD_DIM

    def chunk_body(c, carry):
        r0 = pl.multiple_of(c * SSM_CHUNK, SSM_CHUNK)
        rows = pl.ds(r0, SSM_CHUNK)
        adt = adt_s[rows, :]
        acum = _split_dot_rhs(tri_ref[...], adt, 3)
        acum_t = acum.T
        acum_full = _split_dot_lhs(acum, e16_ref[...], 2)
        alast_full = acum_full[SSM_CHUNK - 1:SSM_CHUNK, :]
        dfs_full = jnp.exp(acum_full)
        dte_full = jnp.exp(alast_full - acum_full)
        cdec_full = jnp.exp(alast_full)
        xdt = xdt_s[rows, :]
        for g in range(SSM_GROUPS):
            gl = slice(g * GROUP_W, (g + 1) * GROUP_W)
            b_g = bc_s[rows, g * SSM_STATE:(g + 1) * SSM_STATE].astype(BF16)
            c_g = bc_s[rows, SSM_GN + g * SSM_STATE:SSM_GN + (g + 1) * SSM_STATE].astype(BF16)
            cb = _dot_nt(c_g, b_g)
            st = state[g]
            y_off = _dot(c_g, st.astype(BF16)) * dfs_full[:, gl]
            xw = (xdt[:, gl] * dte_full[:, gl]).astype(BF16)
            state[g] = st * cdec_full[:, gl] + _dot_tn(b_g, xw)
            for pr in range(GROUP_W // LANES):
                h0 = g * (SSM_HEADS // SSM_GROUPS) + 2 * pr
                pl_ = slice(g * GROUP_W + pr * LANES, g * GROUP_W + (pr + 1) * LANES)
                both = acum_full[:, pl_]
                swapped = pltpu.roll(both, HEAD_DIM, 1)
                per_head = (jnp.where(first_head, both, swapped), jnp.where(first_head, swapped, both))
                ms = []
                for j, h in enumerate((h0, h0 + 1)):
                    seg = per_head[j] - acum_t[h:h + 1, :]
                    dec = jnp.where(causal, jnp.exp(jnp.minimum(seg, 0.0)), 0.0)
                    ms.append((cb * dec).astype(BF16))
                lhs = jnp.concatenate(ms, axis=1)
                xp = xdt[:, pl_]
                rhs = jnp.concatenate([jnp.where(first_head, xp, 0.0),
                                       jnp.where(first_head, 0.0, xp)], axis=0).astype(BF16)
                y_pair = _dot(lhs, rhs) + y_off[:, pr * LANES:(pr + 1) * LANES]
                ybuf[rows, pl_] = y_pair
        return carry

    lax.fori_loop(0, ts // SSM_CHUNK, chunk_body, 0, unroll=True)

    y = ybuf[...] + xs * dskip_ref[...]
    y = y * _silu(_dot(u1, wz_ref[...]))
    parts = []
    for g in range(SSM_GROUPS):
        yg = y[:, g * GROUP_W:(g + 1) * GROUP_W]
        ms = jnp.mean(yg * yg, axis=-1, keepdims=True)
        parts.append(yg * lax.rsqrt(ms + RMS_EPS))
    y = jnp.concatenate(parts, axis=1) * ssnw_ref[...]

    ycat = jnp.concatenate([ysc.astype(BF16), y.astype(BF16)], axis=1)
    mix = _dot(ycat, wout_ref[...])
    out_ref[0] = _layer_norm(alpha * x + (1.0 + gate1) * mix, lng_ref[...], lnb_ref[...])


def _const_spec(shape):
    nd = len(shape)
    return pl.BlockSpec(shape, lambda b, s: (0,) * nd, pipeline_mode=pl.Buffered(1))


def _mixer_call(x, cond3, weights, alpha):
    bsz, seqlen, d = x.shape
    ts = MIX_TS
    kern = functools.partial(_mixer_kernel, alpha, ts)
    in_specs = [pl.BlockSpec((1, ts, d), lambda b, s: (b, s, 0)),
                pl.BlockSpec((1, 1, cond3.shape[2]), lambda b, s: (b, 0, 0))]
    in_specs += [_const_spec(w.shape) for w in weights]
    return pl.pallas_call(
        kern,
        grid=(bsz, seqlen // ts),
        in_specs=in_specs,
        out_specs=pl.BlockSpec((1, ts, d), lambda b, s: (b, s, 0)),
        out_shape=jax.ShapeDtypeStruct((bsz, seqlen, d), F32),
        scratch_shapes=[pltpu.VMEM((ts + CARRY, D_MODEL), F32),
                        pltpu.VMEM((ts + CARRY, SSM_CONV_DIM), F32),
                        pltpu.VMEM((SSM_GROUPS, SSM_STATE, GROUP_W), F32),
                        pltpu.VMEM((ts, D_MODEL), F32),
                        pltpu.VMEM((ts, 2 * SSM_GN), F32),
                        pltpu.VMEM((ts, LANES), F32),
                        pltpu.VMEM((ts, D_MODEL), F32)],
        compiler_params=pltpu.CompilerParams(
            dimension_semantics=("arbitrary", "arbitrary"), vmem_limit_bytes=VMEM_LIMIT),
        name="mixer",
    )(x, cond3, *weights)


def _sort_network(n):
    pairs = []

    def merge(lo, m, r):
        step = 2 * r
        if step < m:
            merge(lo, m, step)
            merge(lo + r, m, step)
            pairs.extend((i, i + r) for i in range(lo + r, lo + m - r, step))
        else:
            pairs.append((lo, lo + r))

    def sort(lo, m):
        if m > 1:
            sort(lo, m // 2)
            sort(lo + m // 2, m // 2)
            merge(lo, m, 1)

    sort(0, n)
    return pairs


_SORT16 = _sort_network(PEER_TOPK)
_L_FULL_ROWS = 3


def _top_values(slabs):
    v = list(slabs)
    for i, j in _SORT16:
        v[i], v[j] = jnp.maximum(v[i], v[j]), jnp.minimum(v[i], v[j])
    vals = []
    for k in range(PEER_TOPK):
        mx = jnp.max(v[0], axis=0, keepdims=True)
        vals.append(mx)
        if k + 1 < PEER_TOPK:
            hit = v[0] == mx
            for r in range(PEER_TOPK - 1 - k):
                v[r] = jnp.where(hit, v[r + 1], v[r])
    return vals


def _slabs(x):
    return [x[r * SUBLANES:(r + 1) * SUBLANES] for r in range(x.shape[0] // SUBLANES)]


def _stack_rows(rows, row_idx):
    out = rows[-1]
    for r in range(len(rows) - 2, -1, -1):
        out = jnp.where(row_idx == r, rows[r], out)
    return out


def _prep_kernel(x1_ref, cond_ref, wqt_ref, k1_ref, k2_ref, u2t_ref, rank2_ref, e2_ref, l1_ref, e1_ref):
    x1 = x1_ref[...]
    cond = cond_ref[0]
    shift2 = cond[:, 3 * D_MODEL:4 * D_MODEL]
    scale2 = cond[:, 4 * D_MODEL:5 * D_MODEL]
    u2t = (x1 * (1.0 + scale2) + shift2).T.astype(BF16)
    u2t_ref[...] = u2t
    q_t = _dot(wqt_ref[...], u2t)
    row16 = lax.broadcasted_iota(jnp.int32, (PEER_TOPK, 1), 0)
    for h in range(PEER_HEADS):
        q1 = q_t[h * PEER_DKEY:h * PEER_DKEY + PEER_HALF].astype(BF16)
        q2 = q_t[h * PEER_DKEY + PEER_HALF:(h + 1) * PEER_DKEY].astype(BF16)
        s1 = _dot(k1_ref[h], q1)
        s2 = _dot(k2_ref[h], q2)
        a = _top_values(_slabs(s1))
        b = _top_values(_slabs(s2))
        a_st = _stack_rows(a, row16)
        b_st = _stack_rows(b, row16)
        pieces = [a[r] + b_st for r in range(4)]
        pieces += [jnp.where(row16 >= 4, b[c] + a_st, NEG_INF) for c in range(3)]
        cand = [sl for pc in pieces for sl in _slabs(pc)]
        cand += [jnp.full(cand[0].shape, NEG_INF, F32)] * (PEER_TOPK - len(cand))
        tau = _top_values(cand)[PEER_TOPK - 1]
        rank2 = jnp.full(s2.shape, float(PEER_TOPK), F32)
        for r in range(PEER_TOPK):
            rank2 = jnp.where(s2 == b[r], float(r), rank2)
        m = a[0] + b[0]
        zsum = None
        for pc in pieces:
            e = jnp.sum(jnp.where(pc >= tau, jnp.exp(pc - m), 0.0), axis=0, keepdims=True)
            zsum = e if zsum is None else zsum + e
        lmap = jnp.zeros(s1.shape, F32)
        for c in range(PEER_TOPK // (_L_FULL_ROWS + 1)):
            lmap = jnp.where(s1 + b[c] >= tau, lmap + 1.0, lmap)
        for r in range(_L_FULL_ROWS):
            cnt = jnp.sum(jnp.where(a[r] + b_st >= tau, 1.0, 0.0), axis=0, keepdims=True)
            lmap = jnp.where(s1 == a[r], cnt, lmap)
        rank2_ref[h] = rank2.astype(BF16)
        e2_ref[h] = jnp.exp(s2 - b[0]).astype(BF16)
        l1_ref[h] = lmap
        e1_ref[h] = jnp.exp(s1 - a[0]) * (0.5 / zsum)


def _prep_call(x1_flat, cond3, wqt, k1, k2, seqlen):
    t, d = x1_flat.shape
    tm = PREP_TM
    per_b = seqlen // tm
    score_spec = pl.BlockSpec((PEER_HEADS, PEER_NKEYS, tm), lambda i: (0, 0, i))
    return pl.pallas_call(
        _prep_kernel,
        grid=(t // tm,),
        in_specs=[pl.BlockSpec((tm, d), lambda i: (i, 0)),
                  pl.BlockSpec((1, 1, cond3.shape[2]), lambda i: (i // per_b, 0, 0)),
                  pl.BlockSpec(wqt.shape, lambda i: (0, 0), pipeline_mode=pl.Buffered(1)),
                  pl.BlockSpec(k1.shape, lambda i: (0, 0, 0), pipeline_mode=pl.Buffered(1)),
                  pl.BlockSpec(k2.shape, lambda i: (0, 0, 0), pipeline_mode=pl.Buffered(1))],
        out_specs=[pl.BlockSpec((d, tm), lambda i: (0, i)),
                   score_spec, score_spec, score_spec, score_spec],
        out_shape=[jax.ShapeDtypeStruct((d, t), BF16),
                   jax.ShapeDtypeStruct((PEER_HEADS, PEER_NKEYS, t), BF16),
                   jax.ShapeDtypeStruct((PEER_HEADS, PEER_NKEYS, t), BF16),
                   jax.ShapeDtypeStruct((PEER_HEADS, PEER_NKEYS, t), F32),
                   jax.ShapeDtypeStruct((PEER_HEADS, PEER_NKEYS, t), F32)],
        compiler_params=pltpu.CompilerParams(
            dimension_semantics=("arbitrary",), vmem_limit_bytes=VMEM_LIMIT),
        name="peer_prep",
    )(x1_flat, cond3, wqt, k1, k2)


def _peer_kernel(alpha, n_sub, sub_i,
                 u2t_ref, rank2_ref, e2_ref, l1_ref, e1_ref, u_ref, vt_ref, x1_ref,
                 cond_ref, lng_ref, lnb_ref, out_ref,
                 acc_ref, h_even, h_odd, w_even, w_odd, act_even, act_odd):
    e_idx = pl.program_id(1)

    @pl.when(e_idx == 0)
    def _():
        acc_ref[...] = jnp.zeros(acc_ref.shape, F32)

    sub = sub_i * PEER_NKEYS
    h_bufs, w_bufs, act_bufs = (h_even, h_odd), (w_even, w_odd), (act_even, act_odd)

    def first_matmul(k):
        h_bufs[k % 2][...] = _dot(u_ref[k * sub:(k + 1) * sub, :], u2t_ref[...])

    def gates(k):
        for ii in range(sub_i):
            i = e_idx * (n_sub * sub_i) + k * sub_i + ii
            w = None
            for h in range(PEER_HEADS):
                lrow = l1_ref[h, pl.ds(i, 1), :].astype(BF16)
                erow = e1_ref[h, pl.ds(i, 1), :].astype(BF16)
                wh = jnp.where(rank2_ref[h] < lrow, e2_ref[h], 0.0) * erow
                w = wh if w is None else w + wh
            w_bufs[k % 2][ii * PEER_NKEYS:(ii + 1) * PEER_NKEYS, :] = w

    def activate(k):
        hh = h_bufs[k % 2][...]
        gelu2 = hh * (1.0 + lax.erf(hh * INV_SQRT2))
        act_bufs[k % 2][...] = gelu2.astype(BF16) * w_bufs[k % 2][...]

    def second_matmul(k):
        acc_ref[...] += _dot(vt_ref[k], act_bufs[k % 2][...])

    first_matmul(0)
    gates(0)
    for k in range(n_sub):
        if k + 1 < n_sub:
            first_matmul(k + 1)
            gates(k + 1)
        activate(k)
        if k >= 1:
            second_matmul(k - 1)
    second_matmul(n_sub - 1)

    @pl.when(e_idx == pl.num_programs(1) - 1)
    def _():
        ffn = acc_ref[...].T
        gate2 = cond_ref[0][:, 5 * D_MODEL:6 * D_MODEL]
        out_ref[...] = _layer_norm(alpha * x1_ref[...] + (1.0 + gate2) * ffn,
                                   lng_ref[...], lnb_ref[...])


def _peer_call(u2t, rank2, e2, l1, e1, u_bf, vt_bf, x1_flat, cond3, ln_g, ln_b, seqlen, alpha):
    d, t = u2t.shape
    n_exp = u_bf.shape[0]
    tm, te = PEER_TM, PEER_TE
    per_b = seqlen // tm
    sub = PEER_SUB_I * PEER_NKEYS
    n_sub = te // sub
    assert vt_bf.shape == (n_exp // sub, d, sub)
    kern = functools.partial(_peer_kernel, alpha, n_sub, PEER_SUB_I)
    score_spec = pl.BlockSpec((PEER_HEADS, PEER_NKEYS, tm), lambda i, e: (0, 0, i))
    stage_f32 = pltpu.VMEM((sub, tm), F32)
    stage_bf16 = pltpu.VMEM((sub, tm), BF16)
    return pl.pallas_call(
        kern,
        grid=(t // tm, n_exp // te),
        in_specs=[pl.BlockSpec((d, tm), lambda i, e: (0, i)),
                  score_spec, score_spec, score_spec, score_spec,
                  pl.BlockSpec((te, d), lambda i, e: (e, 0)),
                  pl.BlockSpec((n_sub, d, sub), lambda i, e: (e, 0, 0)),
                  pl.BlockSpec((tm, d), lambda i, e: (i, 0)),
                  pl.BlockSpec((1, 1, cond3.shape[2]), lambda i, e: (i // per_b, 0, 0)),
                  pl.BlockSpec((1, d), lambda i, e: (0, 0)),
                  pl.BlockSpec((1, d), lambda i, e: (0, 0))],
        out_specs=pl.BlockSpec((tm, d), lambda i, e: (i, 0)),
        out_shape=jax.ShapeDtypeStruct((t, d), F32),
        scratch_shapes=[pltpu.VMEM((d, tm), F32), stage_f32, stage_f32,
                        stage_bf16, stage_bf16, stage_bf16, stage_bf16],
        compiler_params=pltpu.CompilerParams(
            dimension_semantics=("arbitrary", "arbitrary"), vmem_limit_bytes=VMEM_LIMIT),
        name="peer",
    )(u2t, rank2, e2, l1, e1, u_bf, vt_bf, x1_flat, cond3, ln_g, ln_b)


def _mixer_constants():
    ch = jnp.arange(D_MODEL)
    hd = jnp.arange(LANES)
    g16 = (ch[:, None] // HEAD_DIM == hd[None, :]).astype(BF16)
    e16 = (hd[:, None] == ch[None, :] // HEAD_DIM).astype(BF16)
    tt = jnp.arange(SSM_CHUNK)
    tri = (tt[:, None] >= tt[None, :]).astype(BF16)
    return g16, e16, tri


def _row(v):
    return v.reshape(1, -1).astype(F32)


def _pad_lanes(v):
    v = v.reshape(1, -1).astype(F32)
    return jnp.pad(v, ((0, 0), (0, LANES - v.shape[1])))


def _layer(x, cond3, alpha, w_in, sc_conv_w, ssm_conv_w, ssm_conv_b, dt_bias, a_log, d_skip,
           sc_norm_w, ssm_norm_w, w_out, ln1_g, ln1_b, w_query, sub_keys1, sub_keys2,
           expert_u, expert_v, ln2_g, ln2_b):
    bsz, seqlen, d = x.shape
    o_z = 3 * D_MODEL
    o_xbc = o_z + D_MODEL
    o_dt = o_xbc + SSM_CONV_DIM
    w_dt = jnp.pad(w_in[:, o_dt:], ((0, 0), (0, LANES - SSM_HEADS)))
    weights = [w_in[:, :o_z].astype(BF16), w_in[:, o_z:o_xbc].astype(BF16),
               w_in[:, o_xbc:o_dt].astype(BF16), w_dt.astype(BF16),
               sc_conv_w.astype(F32), ssm_conv_w.astype(F32), _row(ssm_conv_b),
               _pad_lanes(dt_bias), _pad_lanes(a_log), _row(jnp.repeat(d_skip, HEAD_DIM)),
               _row(sc_norm_w), _row(ssm_norm_w), w_out.astype(BF16), _row(ln1_g), _row(ln1_b),
               *_mixer_constants()]
    x1 = _mixer_call(x, cond3, weights, alpha)
    x1_flat = x1.reshape(bsz * seqlen, d)
    u2t, rank2, e2, l1, e1 = _prep_call(x1_flat, cond3, w_query.T.astype(BF16),
                                        sub_keys1.astype(BF16), sub_keys2.astype(BF16), seqlen)
    sub = PEER_SUB_I * PEER_NKEYS
    vt = expert_v.reshape(-1, sub, d).transpose(0, 2, 1).astype(BF16)
    out = _peer_call(u2t, rank2, e2, l1, e1, expert_u.astype(BF16), vt,
                     x1_flat, cond3, _row(ln2_g), _row(ln2_b), seqlen, alpha)
    return out.reshape(bsz, seqlen, d)


def kernel(x, c, w_cond, b_cond, w_in, sc_conv_w, ssm_conv_w, ssm_conv_b, dt_bias, a_log, d_skip,
           sc_norm_w, ssm_norm_w, w_out, ln1_g, ln1_b, w_query, sub_keys1, sub_keys2, expert_u,
           expert_v, ln2_g, ln2_b):
    depth = w_cond.shape[0]
    bsz = x.shape[0]
    alpha = (2.0 * depth) ** 0.25
    c_pad = jnp.pad(c, ((0, (-bsz) % CARRY), (0, 0)))
    for l in range(depth):
        cond = _cond_call(c_pad, w_cond[l], b_cond[l].reshape(1, -1))[:bsz]
        cond3 = cond.reshape(bsz, 1, -1)
        x = _layer(x, cond3, alpha, w_in[l], sc_conv_w[l], ssm_conv_w[l], ssm_conv_b[l], dt_bias[l],
                   a_log[l], d_skip[l], sc_norm_w[l], ssm_norm_w[l], w_out[l], ln1_g[l], ln1_b[l],
                   w_query[l], sub_keys1[l], sub_keys2[l], expert_u[l], expert_v[l], ln2_g[l],
                   ln2_b[l])
    return x
```
